```python
import jax, jax.numpy as jnp
from jax import lax
import numpy as np

D_MODEL = 1024
BATCH = 2
SEQ = 8192
DEPTH = 2
DEC_BATCH = 16
DEC_SEQ = 64
PAST_LEN = 1024

CHUNK = 64
RET_HEADS = 4
RET_DK = 64
RET_DV = 128
RET_QK_W = RET_HEADS * RET_DK
RET_V_W = RET_HEADS * RET_DV
CONV_CH = D_MODEL - RET_V_W
CONV_WIDTH = 31
MIX_W = RET_V_W + CONV_CH
IN_COLS = 2 * RET_QK_W + 2 * RET_V_W + 2 * CONV_CH
N_GROUPS = 4
EXPERTS_PER_GROUP = 8
N_EXPERTS = N_GROUPS * EXPERTS_PER_GROUP
TOP_K_IN_GROUP = 2
EXPERT_FF = 512
ROPE_BASE = 10000.0
EPS = 1e-6

kernel_name = 'hybrid_retention_conformer_hmoe_stream_step'


def rmsnorm(x, g):
    xf = x.astype(jnp.float32)
    y = xf * lax.rsqrt(jnp.mean(xf * xf, axis=-1, keepdims=True) + EPS)
    return (y * g.astype(jnp.float32)).astype(x.dtype)


def layernorm(x, g, b):
    xf = x.astype(jnp.float32)
    mu = jnp.mean(xf, axis=-1, keepdims=True)
    xc = xf - mu
    var = jnp.mean(xc * xc, axis=-1, keepdims=True)
    y = xc * lax.rsqrt(var + EPS) * g.astype(jnp.float32) + b.astype(jnp.float32)
    return y.astype(x.dtype)


def rotate(x, pos):
    half = x.shape[-1] // 2
    inv = ROPE_BASE ** (-jnp.arange(half, dtype=jnp.float32) / half)
    ang = pos.astype(jnp.float32)[:, None] * inv[None, :]
    cos = jnp.cos(ang)[None, :, None, :]
    sin = jnp.sin(ang)[None, :, None, :]
    x1, x2 = x[..., :half], x[..., half:]
    return jnp.concatenate([x1 * cos - x2 * sin, x2 * cos + x1 * sin], axis=-1)


def retention(q, k, v, s0, chunk):
    b, l, h, _ = q.shape
    nc = l // chunk
    log_g = jnp.log1p(-jnp.exp2(-5.0 - jnp.arange(h, dtype=jnp.float32)))
    idx = jnp.arange(chunk, dtype=jnp.float32)
    rel = idx[:, None] - idx[None, :]
    intra = jnp.where(rel >= 0, jnp.exp(log_g[:, None, None] * jnp.maximum(rel, 0.0)), 0.0)
    q_dec = jnp.exp(log_g[None, :] * (idx + 1.0)[:, None])[None, :, :, None]
    k_dec = jnp.exp(log_g[None, :] * (chunk - 1.0 - idx)[:, None])[None, :, :, None]
    chunk_dec = jnp.exp(log_g * chunk)[None, :, None, None]

    def step(s, blk):
        qc, kc, vc = blk
        scores = jnp.einsum('bihd,bjhd->bhij', qc, kc) * intra
        o = (jnp.einsum('bhij,bjhe->bihe', scores, vc)
             + jnp.einsum('bihd,bhde->bihe', qc, s) * q_dec)
        s_new = chunk_dec * s + jnp.einsum('bjhd,bjhe->bhde', kc * k_dec, vc)
        return s_new, o

    def blocks(t):
        return jnp.moveaxis(t.reshape(b, nc, chunk, h, t.shape[-1]), 1, 0)

    s_fin, o = lax.scan(step, s0, (blocks(q), blocks(k), blocks(v)))
    o = jnp.moveaxis(o, 0, 1).reshape(b, l, h, v.shape[-1])
    return o, s_fin


def token_mixers(xn, w_in, conv_w, conv_b, ln_g, ln_b, w_out, s0, conv_buf, pos0, chunk):
    f32 = jnp.float32
    b, l, _ = xn.shape
    proj = xn @ w_in
    cuts = [RET_QK_W, 2 * RET_QK_W, 2 * RET_QK_W + RET_V_W, 2 * RET_QK_W + 2 * RET_V_W,
            2 * RET_QK_W + 2 * RET_V_W + CONV_CH]
    q, k, v, g, ca, cb = jnp.split(proj, cuts, axis=-1)
    pos = pos0 + jnp.arange(l)
    q = rotate(q.reshape(b, l, RET_HEADS, RET_DK).astype(f32), pos)
    k = rotate(k.reshape(b, l, RET_HEADS, RET_DK).astype(f32), pos) * (RET_DK ** -0.5)
    v = v.reshape(b, l, RET_HEADS, RET_DV).astype(f32)
    o, s_new = retention(q, k, v, s0.astype(f32), chunk)
    o = o * lax.rsqrt(jnp.mean(o * o, axis=-1, keepdims=True) + EPS)
    o = o.reshape(b, l, RET_V_W).astype(xn.dtype) * jax.nn.silu(g)
    u = ca * jax.nn.sigmoid(cb)
    full = jnp.concatenate([conv_buf.astype(u.dtype), u], axis=1)
    c = lax.conv_general_dilated(full, conv_w[:, None, :].astype(full.dtype), (1,), 'VALID',
                                 dimension_numbers=('NWC', 'WIO', 'NWC'),
                                 feature_group_count=CONV_CH) + conv_b
    new_buf = full[:, -(CONV_WIDTH - 1):]
    c = jax.nn.silu(layernorm(c, ln_g, ln_b))
    y = jnp.concatenate([o, c], axis=-1) @ w_out
    return y, s_new, new_buf


def hier_moe(x, rg_w, rg_b, re_w, re_b, w_gate, w_up, w_down):
    f32 = jnp.float32
    b, l, d = x.shape
    t = x.reshape(b * l, d)
    pg = jax.nn.softmax((t @ rg_w + rg_b).astype(f32), axis=-1)
    gidx = jnp.argmax(pg, axis=-1)
    pgate = jnp.take_along_axis(pg, gidx[:, None], axis=1)[:, 0]
    le = jnp.einsum('td,gde->tge', t, re_w) + re_b
    le = jnp.take_along_axis(le, gidx[:, None, None], axis=1)[:, 0].astype(f32)
    top_v, top_i = lax.top_k(le, TOP_K_IN_GROUP)
    w2 = jax.nn.softmax(top_v, axis=-1) * pgate[:, None]
    eid = gidx[:, None] * EXPERTS_PER_GROUP + top_i
    combine = jnp.sum(jax.nn.one_hot(eid, N_EXPERTS, dtype=f32) * w2[..., None], axis=1).astype(x.dtype)
    y = jnp.zeros_like(t)
    for e in range(N_EXPERTS):
        hdn = jax.nn.silu(t @ w_gate[e]) * (t @ w_up[e])
        y = y + combine[:, e:e + 1] * (hdn @ w_down[e])
    return y.reshape(b, l, d)


def setup_inputs(seed: int = 0) -> dict:
    key = jax.random.key(seed)
    ks = jax.random.split(key, 21)

    def nrm(k, shape, scale):
        return jax.random.normal(k, shape, jnp.float32) * scale

    return {
        'x_prompt': nrm(ks[0], (BATCH, SEQ, D_MODEL), 1.0),
        'x_sample': nrm(ks[1], (DEC_BATCH, DEC_SEQ, D_MODEL), 1.0),
        'state_ret': nrm(ks[2], (DEPTH, DEC_BATCH, RET_HEADS, RET_DK, RET_DV), 0.1),
        'cache_conv': nrm(ks[3], (DEPTH, DEC_BATCH, CONV_WIDTH - 1, CONV_CH), 0.5),
        'norm_mix_g': 1.0 + nrm(ks[4], (DEPTH, D_MODEL), 0.02),
        'w_in': nrm(ks[5], (DEPTH, D_MODEL, IN_COLS), D_MODEL ** -0.5),
        'conv_w': nrm(ks[6], (DEPTH, CONV_WIDTH, CONV_CH), CONV_WIDTH ** -0.5),
        'conv_b': nrm(ks[7], (DEPTH, CONV_CH), 0.02),
        'conv_ln_g': 1.0 + nrm(ks[8], (DEPTH, CONV_CH), 0.02),
        'conv_ln_b': nrm(ks[9], (DEPTH, CONV_CH), 0.02),
        'w_out': nrm(ks[10], (DEPTH, MIX_W, D_MODEL), MIX_W ** -0.5),
        'norm_ffn_g': 1.0 + nrm(ks[11], (DEPTH, D_MODEL), 0.02),
        'router_group_w': nrm(ks[12], (DEPTH, D_MODEL, N_GROUPS), D_MODEL ** -0.5),
        'router_group_b': nrm(ks[13], (DEPTH, N_GROUPS), 0.01),
        'router_expert_w': nrm(ks[14], (DEPTH, N_GROUPS, D_MODEL, EXPERTS_PER_GROUP), D_MODEL ** -0.5),
        'router_expert_b': nrm(ks[15], (DEPTH, N_GROUPS, EXPERTS_PER_GROUP), 0.01),
        'expert_w_gate': nrm(ks[16], (DEPTH, N_EXPERTS, D_MODEL, EXPERT_FF), D_MODEL ** -0.5),
        'expert_w_up': nrm(ks[17], (DEPTH, N_EXPERTS, D_MODEL, EXPERT_FF), D_MODEL ** -0.5),
        'expert_w_down': nrm(ks[18], (DEPTH, N_EXPERTS, EXPERT_FF, D_MODEL), EXPERT_FF ** -0.5),
        'norm_final_g': 1.0 + nrm(ks[19], (D_MODEL,), 0.02),
    }


def reference(x_prompt, x_sample, state_ret, cache_conv, norm_mix_g, w_in, conv_w, conv_b,
              conv_ln_g, conv_ln_b, w_out, norm_ffn_g, router_group_w, router_group_b,
              router_expert_w, router_expert_b, expert_w_gate, expert_w_up, expert_w_down,
              norm_final_g):
    hp, hs = x_prompt, x_sample
    s_zero = jnp.zeros((hp.shape[0], RET_HEADS, RET_DK, RET_DV), jnp.float32)
    buf_zero = jnp.zeros((hp.shape[0], CONV_WIDTH - 1, CONV_CH), hp.dtype)
    ret_p, conv_p, ret_s, conv_s = [], [], [], []
    for i in range(DEPTH):
        mix_w = (w_in[i], conv_w[i], conv_b[i], conv_ln_g[i], conv_ln_b[i], w_out[i])
        yp, sp, bp = token_mixers(rmsnorm(hp, norm_mix_g[i]), *mix_w, s_zero, buf_zero, 0, CHUNK)
        ys, ss, bs = token_mixers(rmsnorm(hs, norm_mix_g[i]), *mix_w, state_ret[i], cache_conv[i],
                                  PAST_LEN, hs.shape[1])
        hp = hp + yp
        hs = hs + ys
        moe_w = (router_group_w[i], router_group_b[i], router_expert_w[i], router_expert_b[i],
                 expert_w_gate[i], expert_w_up[i], expert_w_down[i])
        hp = hp + hier_moe(rmsnorm(hp, norm_ffn_g[i]), *moe_w)
        hs = hs + hier_moe(rmsnorm(hs, norm_ffn_g[i]), *moe_w)
        ret_p.append(sp)
        conv_p.append(bp)
        ret_s.append(ss)
        conv_s.append(bs)
    y_prompt = rmsnorm(hp, norm_final_g)
    y_sample = rmsnorm(hs, norm_final_g)
    return (y_prompt, y_sample, jnp.stack(ret_p), jnp.stack(conv_p), jnp.stack(ret_s), jnp.stack(conv_s))
```

```python
import functools

import jax
import jax.numpy as jnp
from jax import lax
from jax.experimental import pallas as pl
from jax.experimental.pallas import tpu as pltpu

F32 = jnp.float32
BF16 = jnp.bfloat16
I32 = jnp.int32

D_MODEL = 1024
DEPTH = 2
BATCH, SEQ = 2, 8192
DEC_BATCH, DEC_SEQ = 16, 64
PAST_LEN = 1024
RET_HEADS, RET_DK, RET_DV = 4, 64, 128
QK_W = RET_HEADS * RET_DK
V_W = RET_HEADS * RET_DV
CONV_CH = 512
CONV_WIDTH = 31
HIST = 32
IN_COLS = 2 * QK_W + 2 * V_W + 2 * CONV_CH
N_GROUPS, EPG = 4, 8
N_EXPERTS = N_GROUPS * EPG
EXPERT_FF = 512
ROPE_BASE = 10000.0
EPS = 1e-6

T_PROMPT = BATCH * SEQ
T_SAMPLE = DEC_BATCH * DEC_SEQ
T_ALL = T_PROMPT + T_SAMPLE

TM = 256
C_PROMPT = 256
TR = 512
TE = 256
N_SLOTS = 2 * T_ALL
MAX_TILES = N_SLOTS // TE + N_EXPERTS
P_ROWS = MAX_TILES * TE
VMEM_LIMIT = 48 * 1024 * 1024
N_PROMPT_STEPS = T_PROMPT // TM

_NT = (((1,), (1,)), ((), ()))
_TN = (((0,), (0,)), ((), ()))


def _params(n_axes=1):
    return pltpu.CompilerParams(dimension_semantics=("arbitrary",) * n_axes,
                                vmem_limit_bytes=VMEM_LIMIT)


def _const(shape):
    nd = len(shape)
    return pl.BlockSpec(shape, lambda *_: (0,) * nd)


def _mixer_body(c, ns, carry, steps_per_seq,
                x_ref, g_ref, win_ref, cos_ref, sin_ref, intra_ref, qdec_ref, kdec_ref, decs_ref,
                masks_ref, cw_ref, cb_ref, lng_ref, lnb_ref, wout_ref, s0_ref, c0_ref,
                out_ref, snew_ref, cnew_ref, s_scr, ubuf, cv_scr, cat_scr):
    step = pl.program_id(0)
    x = x_ref[...]
    xn = (x * lax.rsqrt(jnp.mean(x * x, axis=-1, keepdims=True) + EPS) * g_ref[...]).astype(BF16)

    def proj(lo, hi):
        return jnp.dot(xn, win_ref[:, lo:hi], preferred_element_type=F32)

    q = proj(0, QK_W)
    k = proj(QK_W, 2 * QK_W)
    vb = proj(2 * QK_W, 2 * QK_W + V_W).astype(BF16)
    gate = proj(2 * QK_W + V_W, 2 * QK_W + 2 * V_W)
    ca = proj(2 * QK_W + 2 * V_W, 2 * QK_W + 2 * V_W + CONV_CH)
    cb = proj(2 * QK_W + 2 * V_W + CONV_CH, IN_COLS)

    lane = lax.broadcasted_iota(I32, (TM, QK_W), 1)
    first_half = (lane & (RET_DK // 2)) == 0

    def rope(t):
        partner = jnp.where(first_half, pltpu.roll(t, QK_W - RET_DK // 2, 1), pltpu.roll(t, RET_DK // 2, 1))
        return t * cos_ref[...] + partner * sin_ref[...]

    q = rope(q)
    k = rope(k) * (RET_DK ** -0.5)

    if carry:
        @pl.when(step % steps_per_seq == 0)
        def _():
            s_scr[...] = jnp.zeros_like(s_scr)
            ubuf[0:HIST, :] = jnp.zeros((HIST, CONV_CH), F32)
    else:
        s_scr[...] = jnp.zeros_like(s_scr)

    lane_c = lax.broadcasted_iota(I32, (c, QK_W), 1)
    for s in range(ns):
        r0 = s * c
        qc, kc, vc = q[r0:r0 + c], k[r0:r0 + c], vb[r0:r0 + c]
        if not carry:
            for h in range(RET_HEADS):
                s_scr[h * RET_DK:(h + 1) * RET_DK, h * RET_DV:(h + 1) * RET_DV] = s0_ref[s, h]
        state = s_scr[...]
        kb = kc.astype(BF16)
        o_cross = jnp.dot(qc.astype(BF16), state.astype(BF16), preferred_element_type=F32) * qdec_ref[...]
        heads = []
        for h in range(RET_HEADS):
            qh = jnp.where((lane_c >> 6) == h, qc, 0.0).astype(BF16)
            scores = lax.dot_general(qh, kb, _NT, preferred_element_type=F32) * intra_ref[h]
            heads.append(jnp.dot(scores.astype(BF16), vc[:, h * RET_DV:(h + 1) * RET_DV],
                                 preferred_element_type=F32))
        o = jnp.concatenate(heads, axis=1) + o_cross
        kd = (kc * kdec_ref[...]).astype(BF16)
        kv = lax.dot_general(kd, vc, _TN, preferred_element_type=F32)
        s_new = decs_ref[...] * state + masks_ref[...] * kv
        if carry:
            s_scr[...] = s_new

            @pl.when(step % steps_per_seq == steps_per_seq - 1)
            def _():
                for h in range(RET_HEADS):
                    snew_ref[0, h] = s_new[h * RET_DK:(h + 1) * RET_DK, h * RET_DV:(h + 1) * RET_DV]
        else:
            for h in range(RET_HEADS):
                snew_ref[s, h] = s_new[h * RET_DK:(h + 1) * RET_DK, h * RET_DV:(h + 1) * RET_DV]

        normed = []
        for h in range(RET_HEADS):
            oh = o[:, h * RET_DV:(h + 1) * RET_DV]
            normed.append(oh * lax.rsqrt(jnp.mean(oh * oh, axis=-1, keepdims=True) + EPS))
        gc = gate[r0:r0 + c]
        og = jnp.concatenate(normed, axis=1) * (gc * jax.nn.sigmoid(gc))

        u = ca[r0:r0 + c] * jax.nn.sigmoid(cb[r0:r0 + c])
        if not carry:
            ubuf[0:HIST, :] = c0_ref[s]
        ubuf[HIST:HIST + c, :] = u
        off = HIST - (CONV_WIDTH - 1)
        for rb in range(0, c, 32):
            acc = jnp.broadcast_to(cb_ref[...], (32, CONV_CH))
            for j in range(CONV_WIDTH):
                acc = acc + ubuf[rb + off + j:rb + off + j + 32, :] * cw_ref[j:j + 1, :]
            cv_scr[rb:rb + 32, :] = acc
        tail = ubuf[c:c + HIST, :]
        if carry:
            ubuf[0:HIST, :] = tail

            @pl.when(step % steps_per_seq == steps_per_seq - 1)
            def _():
                cnew_ref[0] = tail
        else:
            cnew_ref[s] = tail
        cv = cv_scr[0:c, :]
        mu = jnp.mean(cv, axis=-1, keepdims=True)
        xc = cv - mu
        var = jnp.mean(xc * xc, axis=-1, keepdims=True)
        cn = xc * lax.rsqrt(var + EPS) * lng_ref[...] + lnb_ref[...]
        cact = cn * jax.nn.sigmoid(cn)
        cat_scr[r0:r0 + c, 0:V_W] = og.astype(BF16)
        cat_scr[r0:r0 + c, V_W:D_MODEL] = cact.astype(BF16)

    out_ref[...] = x + jnp.dot(cat_scr[...], wout_ref[...], preferred_element_type=F32)


def _mixer_kernel(x_ref, g_ref, win_ref, cw_ref, cb_ref, lng_ref, lnb_ref, wout_ref, masks_ref,
                  cos_p, sin_p, intra_p, qdec_p, kdec_p, decs_p,
                  cos_s, sin_s, intra_s, qdec_s, kdec_s, decs_s, s0_ref, c0_ref,
                  out_ref, snew_p, cnew_p, snew_s, cnew_s, s_scr, ubuf, cv_scr, cat_scr):
    step = pl.program_id(0)

    @pl.when(step < N_PROMPT_STEPS)
    def _():
        _mixer_body(C_PROMPT, TM // C_PROMPT, True, SEQ // TM,
                    x_ref, g_ref, win_ref, cos_p, sin_p, intra_p, qdec_p, kdec_p, decs_p,
                    masks_ref, cw_ref, cb_ref, lng_ref, lnb_ref, wout_ref, None, None,
                    out_ref, snew_p, cnew_p, s_scr, ubuf, cv_scr, cat_scr)

    @pl.when(step >= N_PROMPT_STEPS)
    def _():
        _mixer_body(DEC_SEQ, TM // DEC_SEQ, False, 1,
                    x_ref, g_ref, win_ref, cos_s, sin_s, intra_s, qdec_s, kdec_s, decs_s,
                    masks_ref, cw_ref, cb_ref, lng_ref, lnb_ref, wout_ref, s0_ref, c0_ref,
                    out_ref, snew_s, cnew_s, s_scr, ubuf, cv_scr, cat_scr)


def _retention_tables(c, pos0):
    h = jnp.arange(RET_HEADS, dtype=F32)
    log_g = jnp.log1p(-jnp.exp2(-5.0 - h))
    idx = jnp.arange(c, dtype=F32)
    rel = idx[:, None] - idx[None, :]
    intra = jnp.where(rel >= 0, jnp.exp(log_g[:, None, None] * jnp.maximum(rel, 0.0)), 0.0)
    q_dec = jnp.exp(log_g[None, :] * (idx + 1.0)[:, None])
    k_dec = jnp.exp(log_g[None, :] * (c - 1.0 - idx)[:, None])
    chunk_dec = jnp.exp(log_g * c)
    qdec = jnp.repeat(q_dec, RET_DV, axis=1)
    kdec = jnp.repeat(k_dec, RET_DK, axis=1)
    row_h = jnp.arange(QK_W) // RET_DK
    col_h = jnp.arange(V_W) // RET_DV
    masks = (row_h[:, None] == col_h[None, :]).astype(F32)
    decs = masks * chunk_dec[row_h][:, None]
    return intra, qdec, kdec, decs, masks


def _rope_tables(pos):
    half = RET_DK // 2
    inv = ROPE_BASE ** (-jnp.arange(half, dtype=F32) / half)
    ang = pos.astype(F32)[:, None] * inv[None, :]
    cos, sin = jnp.cos(ang), jnp.sin(ang)
    cos_t = jnp.tile(jnp.concatenate([cos, cos], axis=1), (1, RET_HEADS))
    sin_t = jnp.tile(jnp.concatenate([-sin, sin], axis=1), (1, RET_HEADS))
    return cos_t, sin_t


def _mixer_layer(h_all, g, win, cw, cb, lng, lnb, wout, s0, c0, tabs_p, tabs_s):
    cos_p, sin_p, intra_p, qdec_p, kdec_p, decs_p, masks = tabs_p
    cos_s, sin_s, intra_s, qdec_s, kdec_s, decs_s, _ = tabs_s
    steps = SEQ // TM
    ns = TM // DEC_SEQ

    def table_specs(c):
        return [_const((RET_HEADS, c, c)), _const((c, V_W)), _const((c, QK_W)), _const((QK_W, V_W))]

    def sample_blk(i):
        return jnp.maximum(i - N_PROMPT_STEPS, 0)

    def prompt_blk(i):
        return jnp.minimum(i // steps, BATCH - 1)

    def prompt_pos(i):
        return (jnp.minimum(i, N_PROMPT_STEPS - 1) % steps, 0)

    return pl.pallas_call(
        _mixer_kernel,
        grid=(T_ALL // TM,),
        in_specs=[pl.BlockSpec((TM, D_MODEL), lambda i: (i, 0)), _const((1, D_MODEL)), _const((D_MODEL, IN_COLS)),
                  _const((HIST, CONV_CH)), _const((1, CONV_CH)), _const((1, CONV_CH)), _const((1, CONV_CH)),
                  _const((D_MODEL, D_MODEL)), _const((QK_W, V_W)),
                  pl.BlockSpec((TM, QK_W), prompt_pos), pl.BlockSpec((TM, QK_W), prompt_pos)]
        + table_specs(C_PROMPT) + [_const((TM, QK_W)), _const((TM, QK_W))] + table_specs(DEC_SEQ)
        + [pl.BlockSpec((ns, RET_HEADS, RET_DK, RET_DV), lambda i: (sample_blk(i), 0, 0, 0)),
           pl.BlockSpec((ns, HIST, CONV_CH), lambda i: (sample_blk(i), 0, 0))],
        out_specs=[pl.BlockSpec((TM, D_MODEL), lambda i: (i, 0)),
                   pl.BlockSpec((1, RET_HEADS, RET_DK, RET_DV), lambda i: (prompt_blk(i), 0, 0, 0)),
                   pl.BlockSpec((1, HIST, CONV_CH), lambda i: (prompt_blk(i), 0, 0)),
                   pl.BlockSpec((ns, RET_HEADS, RET_DK, RET_DV), lambda i: (sample_blk(i), 0, 0, 0)),
                   pl.BlockSpec((ns, HIST, CONV_CH), lambda i: (sample_blk(i), 0, 0))],
        out_shape=[jax.ShapeDtypeStruct((T_ALL, D_MODEL), F32),
                   jax.ShapeDtypeStruct((BATCH, RET_HEADS, RET_DK, RET_DV), F32),
                   jax.ShapeDtypeStruct((BATCH, HIST, CONV_CH), F32),
                   jax.ShapeDtypeStruct((DEC_BATCH, RET_HEADS, RET_DK, RET_DV), F32),
                   jax.ShapeDtypeStruct((DEC_BATCH, HIST, CONV_CH), F32)],
        scratch_shapes=[pltpu.VMEM((QK_W, V_W), F32), pltpu.VMEM((HIST + TM, CONV_CH), F32),
                        pltpu.VMEM((TM, CONV_CH), F32), pltpu.VMEM((TM, D_MODEL), BF16)],
        compiler_params=_params(),
        name="mixer",
    )(h_all, g, win, cw, cb, lng, lnb, wout, masks, cos_p, sin_p, intra_p, qdec_p, kdec_p, decs_p,
      cos_s, sin_s, intra_s, qdec_s, kdec_s, decs_s, s0, c0)


def _rmsnorm(x, g):
    return x * lax.rsqrt(jnp.mean(x * x, axis=-1, keepdims=True) + EPS) * g


def _router_kernel(h_ref, g_ref, wr_ref, br_ref, eid_ref, wcol_ref):
    xn = _rmsnorm(h_ref[...], g_ref[...])
    logits = lax.dot_general(wr_ref[...], xn, _NT, preferred_element_type=F32,
                             precision=lax.Precision.HIGHEST) + br_ref[...]
    best = logits[0:1, :]
    gidx = jnp.zeros((1, TR), I32)
    for r in range(1, N_GROUPS):
        row = logits[r:r + 1, :]
        upd = row > best
        gidx = jnp.where(upd, r, gidx)
        best = jnp.where(upd, row, best)
    denom = jnp.zeros((1, TR), F32)
    for r in range(N_GROUPS):
        denom = denom + jnp.exp(logits[r:r + 1, :] - best)
    pgate = 1.0 / denom
    le = logits[EPG:2 * EPG, :]
    for gi in range(1, N_GROUPS):
        le = jnp.where(gidx == gi, logits[EPG * (gi + 1):EPG * (gi + 2), :], le)
    sub = lax.broadcasted_iota(I32, (EPG, TR), 0)
    v1 = jnp.max(le, axis=0, keepdims=True)
    i1 = jnp.min(jnp.where(le == v1, sub, EPG), axis=0, keepdims=True)
    le2 = jnp.where(sub == i1, -jnp.inf, le)
    v2 = jnp.max(le2, axis=0, keepdims=True)
    i2 = jnp.min(jnp.where(le2 == v2, sub, EPG), axis=0, keepdims=True)
    e2 = jnp.exp(v2 - v1)
    den2 = 1.0 + e2
    wa = (1.0 / den2) * pgate
    wb = (e2 / den2) * pgate
    eid_ref[0:1, :] = gidx * EPG + i1
    eid_ref[1:2, :] = gidx * EPG + i2
    wrows = jnp.concatenate([wa, wb, jnp.zeros((126, TR), F32)], axis=0)
    wcol_ref[...] = wrows.T


def _plan_kernel(eid_ref, upper_ref, ltri_ref, pos_ref, te_ref, nt_ref, cnt_scr, run_scr):
    phase = pl.program_id(0)
    i = pl.program_id(1)
    rows = lax.broadcasted_iota(I32, (N_EXPERTS, TR), 0)
    oh0 = (rows == eid_ref[0:1, :]).astype(F32)
    oh1 = (rows == eid_ref[1:2, :]).astype(F32)
    tot0 = jnp.sum(oh0, axis=1, keepdims=True)
    tot1 = jnp.sum(oh1, axis=1, keepdims=True)

    @pl.when((phase == 0) & (i == 0))
    def _():
        cnt_scr[...] = jnp.zeros_like(cnt_scr)

    @pl.when(phase == 0)
    def _():
        cnt_scr[...] = cnt_scr[...] + (tot0 + tot1)

    @pl.when((phase == 1) & (i == 0))
    def _():
        cnt = cnt_scr[...].astype(I32)
        ntile = ((cnt + (TE - 1)) >> (TE.bit_length() - 1)).astype(F32)
        incl = jnp.dot(ltri_ref[...], ntile, preferred_element_type=F32,
                       precision=lax.Precision.HIGHEST)
        run_scr[...] = (incl - ntile) * float(TE)
        ntot = incl[N_EXPERTS - 1:N_EXPERTS, 0:1]
        j = lax.broadcasted_iota(I32, (N_EXPERTS, 256), 1).astype(F32)
        j = jnp.minimum(j, ntot - 1.0)
        te = jnp.sum((incl[:, 0:1] <= j).astype(F32), axis=0, keepdims=True)
        te_ref[...] = te.astype(I32)
        nt_ref[...] = incl[N_EXPERTS - 1:N_EXPERTS, :].astype(I32)

    @pl.when(phase == 1)
    def _():
        run = run_scr[:, 0:1]
        ex0 = jnp.dot(oh0.astype(BF16), upper_ref[...], preferred_element_type=F32)
        ex1 = jnp.dot(oh1.astype(BF16), upper_ref[...], preferred_element_type=F32)
        p0 = jnp.sum(oh0 * (run + ex0), axis=0, keepdims=True)
        p1 = jnp.sum(oh1 * (run + tot0 + ex1), axis=0, keepdims=True)
        pos_ref[0:1, :] = p0.astype(I32)
        pos_ref[1:2, :] = p1.astype(I32)
        run_scr[...] = run_scr[...] + (tot0 + tot1)


def _dispatch_kernel(pos_ref, h_ref, g_ref, xs_in_ref, xs_ref, xn_scr, sem):
    del xs_in_ref
    xn_scr[...] = _rmsnorm(h_ref[...], g_ref[...])

    def issue(r, carry):
        for kk in range(2):
            pltpu.make_async_copy(xn_scr.at[pl.ds(r, 1)], xs_ref.at[pl.ds(pos_ref[0, kk, r], 1)], sem).start()
        return carry

    lax.fori_loop(0, TM, issue, 0)

    def drain(r, carry):
        for kk in range(2):
            pltpu.make_async_copy(xn_scr.at[pl.ds(0, 1)], xs_ref.at[pl.ds(0, 1)], sem).wait()
        return carry

    lax.fori_loop(0, TM, drain, 0)


def _ffn_kernel(te_ref, nt_ref, xs_ref, wg_ref, wu_ref, wd_ref, ys_ref, wg_scr, wu_scr, wd_scr):
    i = pl.program_id(0)
    first = (i == 0) | (te_ref[i] != te_ref[jnp.maximum(i - 1, 0)])

    @pl.when(first)
    def _():
        wg_scr[...] = wg_ref[0].astype(BF16)
        wu_scr[...] = wu_ref[0].astype(BF16)
        wd_scr[...] = wd_ref[0].astype(BF16)

    @pl.when(i < nt_ref[0])
    def _():
        x = xs_ref[...].astype(BF16)
        a = jnp.dot(x, wg_scr[...], preferred_element_type=F32)
        b = jnp.dot(x, wu_scr[...], preferred_element_type=F32)
        hdn = (a * jax.nn.sigmoid(a)) * b
        ys_ref[...] = jnp.dot(hdn.astype(BF16), wd_scr[...], preferred_element_type=F32)

    @pl.when(i >= nt_ref[0])
    def _():
        ys_ref[...] = jnp.zeros_like(ys_ref)


def _combine_kernel(final, pos_ref, h_ref, wcol_ref, gf_ref, ys_ref, out_ref, ya_scr, yb_scr, sem):
    def issue(r, carry):
        pltpu.make_async_copy(ys_ref.at[pl.ds(pos_ref[0, 0, r], 1)], ya_scr.at[pl.ds(r, 1)], sem).start()
        pltpu.make_async_copy(ys_ref.at[pl.ds(pos_ref[0, 1, r], 1)], yb_scr.at[pl.ds(r, 1)], sem).start()
        return carry

    lax.fori_loop(0, TM, issue, 0)

    def drain(r, carry):
        for _ in range(2):
            pltpu.make_async_copy(ys_ref.at[pl.ds(0, 1)], ya_scr.at[pl.ds(0, 1)], sem).wait()
        return carry

    lax.fori_loop(0, TM, drain, 0)
    w = wcol_ref[...]
    out = h_ref[...] + (w[:, 0:1] * ya_scr[...] + w[:, 1:2] * yb_scr[...])
    if final:
        out = _rmsnorm(out, gf_ref[...])
    out_ref[...] = out


def _moe_layer(h_all, g, wr, br, wg, wu, wd, gf, upper, ltri, final):
    n_r = T_ALL // TR
    eid, wcol = pl.pallas_call(
        _router_kernel,
        grid=(n_r,),
        in_specs=[pl.BlockSpec((TR, D_MODEL), lambda i: (i, 0)), _const((1, D_MODEL)),
                  _const((40, D_MODEL)), _const((40, 1))],
        out_specs=[pl.BlockSpec((2, TR), lambda i: (0, i)), pl.BlockSpec((TR, 128), lambda i: (i, 0))],
        out_shape=[jax.ShapeDtypeStruct((2, T_ALL), I32), jax.ShapeDtypeStruct((T_ALL, 128), F32)],
        compiler_params=_params(),
        name="router",
    )(h_all, g, wr, br)

    pos, te, nt = pl.pallas_call(
        _plan_kernel,
        grid=(2, n_r),
        in_specs=[pl.BlockSpec((2, TR), lambda p, i: (0, i)), _const((TR, TR)), _const((N_EXPERTS, N_EXPERTS))],
        out_specs=[pl.BlockSpec((2, TR), lambda p, i: (0, i * p)), _const((1, 256)), _const((1, 128))],
        out_shape=[jax.ShapeDtypeStruct((2, T_ALL), I32), jax.ShapeDtypeStruct((1, 256), I32),
                   jax.ShapeDtypeStruct((1, 128), I32)],
        scratch_shapes=[pltpu.VMEM((N_EXPERTS, 128), F32), pltpu.VMEM((N_EXPERTS, 128), F32)],
        compiler_params=_params(2),
        name="plan",
    )(eid, upper, ltri)

    n_t = T_ALL // TM
    pos3 = pos.reshape(2, n_t, TM).transpose(1, 0, 2)
    smem_pos = pl.BlockSpec((1, 2, TM), lambda i: (i, 0, 0), memory_space=pltpu.SMEM)

    xs = pl.pallas_call(
        _dispatch_kernel,
        grid=(n_t,),
        in_specs=[smem_pos, pl.BlockSpec((TM, D_MODEL), lambda i: (i, 0)), _const((1, D_MODEL)),
                  pl.BlockSpec(memory_space=pl.ANY)],
        out_specs=pl.BlockSpec(memory_space=pl.ANY),
        out_shape=jax.ShapeDtypeStruct((P_ROWS, D_MODEL), F32),
        scratch_shapes=[pltpu.VMEM((TM, D_MODEL), F32), pltpu.SemaphoreType.DMA(())],
        input_output_aliases={3: 0},
        compiler_params=_params(),
        name="dispatch",
    )(pos3, h_all, g, jnp.zeros((P_ROWS, D_MODEL), F32))

    def row_map(i, te_ref, nt_ref):
        return (jnp.minimum(i, nt_ref[0] - 1), 0)

    ys = pl.pallas_call(
        _ffn_kernel,
        grid_spec=pltpu.PrefetchScalarGridSpec(
            num_scalar_prefetch=2,
            grid=(MAX_TILES,),
            in_specs=[pl.BlockSpec((TE, D_MODEL), row_map),
                      pl.BlockSpec((1, D_MODEL, EXPERT_FF), lambda i, te_ref, nt_ref: (te_ref[i], 0, 0)),
                      pl.BlockSpec((1, D_MODEL, EXPERT_FF), lambda i, te_ref, nt_ref: (te_ref[i], 0, 0)),
                      pl.BlockSpec((1, EXPERT_FF, D_MODEL), lambda i, te_ref, nt_ref: (te_ref[i], 0, 0))],
            out_specs=pl.BlockSpec((TE, D_MODEL), lambda i, te_ref, nt_ref: (i, 0)),
            scratch_shapes=[pltpu.VMEM((D_MODEL, EXPERT_FF), BF16), pltpu.VMEM((D_MODEL, EXPERT_FF), BF16),
                            pltpu.VMEM((EXPERT_FF, D_MODEL), BF16)]),
        out_shape=jax.ShapeDtypeStruct((P_ROWS, D_MODEL), F32),
        compiler_params=_params(),
        name="expert_ffn",
    )(te.reshape(256), nt[0, 0:1], xs, wg, wu, wd)

    out = pl.pallas_call(
        functools.partial(_combine_kernel, final),
        grid=(n_t,),
        in_specs=[smem_pos, pl.BlockSpec((TM, D_MODEL), lambda i: (i, 0)),
                  pl.BlockSpec((TM, 128), lambda i: (i, 0)), _const((1, D_MODEL)),
                  pl.BlockSpec(memory_space=pl.ANY)],
        out_specs=pl.BlockSpec((TM, D_MODEL), lambda i: (i, 0)),
        out_shape=jax.ShapeDtypeStruct((T_ALL, D_MODEL), F32),
        scratch_shapes=[pltpu.VMEM((TM, D_MODEL), F32), pltpu.VMEM((TM, D_MODEL), F32),
                        pltpu.SemaphoreType.DMA(())],
        compiler_params=_params(),
        name="combine",
    )(pos3, h_all, wcol, gf, ys)
    return out


def kernel(x_prompt, x_sample, state_ret, cache_conv, norm_mix_g, w_in, conv_w, conv_b, conv_ln_g, conv_ln_b,
           w_out, norm_ffn_g, router_group_w, router_group_b, router_expert_w, router_expert_b, expert_w_gate,
           expert_w_up, expert_w_down, norm_final_g):
    h = jnp.concatenate([x_prompt.reshape(T_PROMPT, D_MODEL), x_sample.reshape(T_SAMPLE, D_MODEL)], axis=0)

    cos_p, sin_p = _rope_tables(jnp.arange(SEQ))
    cos_s, sin_s = _rope_tables(PAST_LEN + jnp.arange(DEC_SEQ))
    reps = TM // DEC_SEQ
    tabs_p = (cos_p, sin_p) + _retention_tables(C_PROMPT, 0)
    tabs_s = (jnp.tile(cos_s, (reps, 1)), jnp.tile(sin_s, (reps, 1))) + _retention_tables(DEC_SEQ, PAST_LEN)

    upper = (jnp.arange(TR)[:, None] < jnp.arange(TR)[None, :]).astype(BF16)
    ltri = (jnp.arange(N_EXPERTS)[:, None] >= jnp.arange(N_EXPERTS)[None, :]).astype(F32)
    pad_hist = HIST - (CONV_WIDTH - 1)
    gf = norm_final_g.reshape(1, D_MODEL)

    ret_p, conv_p, ret_s, conv_s = [], [], [], []
    for l in range(DEPTH):
        cw = jnp.pad(conv_w[l], ((0, HIST - CONV_WIDTH), (0, 0)))
        c0 = jnp.pad(cache_conv[l], ((0, 0), (pad_hist, 0), (0, 0)))
        h, sp, cp, ss, cs = _mixer_layer(
            h, norm_mix_g[l].reshape(1, D_MODEL), w_in[l].astype(BF16), cw, conv_b[l].reshape(1, CONV_CH),
            conv_ln_g[l].reshape(1, CONV_CH), conv_ln_b[l].reshape(1, CONV_CH), w_out[l].astype(BF16),
            state_ret[l], c0, tabs_p, tabs_s)
        ret_p.append(sp)
        conv_p.append(cp[:, pad_hist:])
        ret_s.append(ss)
        conv_s.append(cs[:, pad_hist:])

        wr = jnp.zeros((40, D_MODEL), F32)
        wr = wr.at[0:N_GROUPS].set(router_group_w[l].T)
        wr = wr.at[EPG:].set(router_expert_w[l].transpose(0, 2, 1).reshape(N_EXPERTS, D_MODEL))
        br = jnp.zeros((40, 1), F32)
        br = br.at[0:N_GROUPS, 0].set(router_group_b[l])
        br = br.at[EPG:, 0].set(router_expert_b[l].reshape(N_EXPERTS))
        h = _moe_layer(h, norm_ffn_g[l].reshape(1, D_MODEL), wr, br, expert_w_gate[l], expert_w_up[l],
                       expert_w_down[l], gf, upper, ltri, final=(l == DEPTH - 1))

    y_prompt = h[:T_PROMPT].reshape(BATCH, SEQ, D_MODEL)
    y_sample = h[T_PROMPT:].reshape(DEC_BATCH, DEC_SEQ, D_MODEL)
    return (y_prompt, y_sample, jnp.stack(ret_p), jnp.stack(conv_p), jnp.stack(ret_s), jnp.stack(conv_s))
```

```python
import functools

import jax
import jax.numpy as jnp
from jax import lax
from jax.experimental import pallas as pl
from jax.experimental.pallas import tpu as pltpu

F32 = jnp.float32
BF16 = jnp.bfloat16
I32 = jnp.int32

D_MODEL = 1024
DEPTH = 2
BATCH, SEQ = 2, 8192
DEC_BATCH, DEC_SEQ = 16, 64
PAST_LEN = 1024
RET_HEADS, RET_DK, RET_DV = 4, 64, 128
QK_W = RET_HEADS * RET_DK
V_W = RET_HEADS * RET_DV
CONV_CH = 512
CONV_WIDTH = 31
HIST = 32
IN_COLS = 2 * QK_W + 2 * V_W + 2 * CONV_CH
N_GROUPS, EPG = 4, 8
N_EXPERTS = N_GROUPS * EPG
EXPERT_FF = 512
ROPE_BASE = 10000.0
EPS = 1e-6

T_PROMPT = BATCH * SEQ
T_SAMPLE = DEC_BATCH * DEC_SEQ
T_ALL = T_PROMPT + T_SAMPLE

SUB, LANES = 8, 128
TM = 256
C_PROMPT = 256
TR = 512
TE = 256
SEG = 32
N_SLOTS = 2 * T_ALL
MAX_TILES = N_SLOTS // TE + N_EXPERTS
P_ROWS = MAX_TILES * TE
VMEM_LIMIT = 48 * 1024 * 1024
N_PROMPT_STEPS = T_PROMPT // TM

SPLIT_FACTOR = float(2 ** 16 + 1)
_NN = (((1,), (0,)), ((), ()))
_NT = (((1,), (1,)), ((), ()))
_TN = (((0,), (0,)), ((), ()))


def _params(n_axes=1):
    return pltpu.CompilerParams(dimension_semantics=("arbitrary",) * n_axes,
                                vmem_limit_bytes=VMEM_LIMIT)


def _const(shape):
    nd = len(shape)
    return pl.BlockSpec(shape, lambda *_: (0,) * nd)


def _load_rt(ref, rows):
    return jnp.concatenate([ref[pl.ds(s, rows, stride=SUB), :] for s in range(SUB)], axis=1)


def _store_rt(ref, val, rows):
    for s in range(SUB):
        ref[pl.ds(s, rows, stride=SUB), :] = val[:, s * LANES:(s + 1) * LANES]


def _split(a):
    t = a * SPLIT_FACTOR
    hi = t - (t - a)
    return hi.astype(BF16), (a - hi).astype(BF16)


def _mm(a, b, dims, precise):
    if not precise:
        return lax.dot_general(a.astype(BF16), b.astype(BF16), dims, preferred_element_type=F32)
    ah, al = _split(a)
    bh, bl = _split(b)
    return (lax.dot_general(ah, bh, dims, preferred_element_type=F32)
            + (lax.dot_general(al, bh, dims, preferred_element_type=F32)
               + lax.dot_general(ah, bl, dims, preferred_element_type=F32)))


def _mm_w(a, whi, wlo):
    if wlo is None:
        return jnp.dot(a.astype(BF16), whi, preferred_element_type=F32)
    ah, al = _split(a)
    return (jnp.dot(ah, whi, preferred_element_type=F32)
            + (jnp.dot(al, whi, preferred_element_type=F32) + jnp.dot(ah, wlo, preferred_element_type=F32)))


def _mixer_body(c, ns, carry, steps_per_seq, x_is_rt, precise,
                x_ref, g_ref, win_ref, winlo_ref, cos_ref, sin_ref, intra_ref, qdec_ref, kdec_ref, decs_ref,
                masks_ref, cw_ref, cb_ref, lng_ref, lnb_ref, wout_ref, woutlo_ref, s0_ref, c0_ref,
                out_ref, snew_ref, cnew_ref, s_scr, ubuf, p_scr, cv_scr, cat_scr):
    step = pl.program_id(0)
    x = _load_rt(x_ref, TM) if x_is_rt else x_ref[...]
    xn = x * lax.rsqrt(jnp.mean(x * x, axis=-1, keepdims=True) + EPS) * g_ref[...]
    xn_parts = _split(xn) if precise else (xn.astype(BF16), None)

    def proj(lo, hi):
        out = jnp.dot(xn_parts[0], win_ref[:, lo:hi], preferred_element_type=F32)
        if precise:
            out = out + (jnp.dot(xn_parts[1], win_ref[:, lo:hi], preferred_element_type=F32)
                         + jnp.dot(xn_parts[0], winlo_ref[:, lo:hi], preferred_element_type=F32))
        return out

    q = proj(0, QK_W)
    k = proj(QK_W, 2 * QK_W)
    v = proj(2 * QK_W, 2 * QK_W + V_W)
    gate = proj(2 * QK_W + V_W, 2 * QK_W + 2 * V_W)
    ca = proj(2 * QK_W + 2 * V_W, 2 * QK_W + 2 * V_W + CONV_CH)
    cb = proj(2 * QK_W + 2 * V_W + CONV_CH, IN_COLS)

    lane = lax.broadcasted_iota(I32, (TM, QK_W), 1)
    first_half = (lane & (RET_DK // 2)) == 0

    def rope(t):
        partner = jnp.where(first_half, pltpu.roll(t, QK_W - RET_DK // 2, 1), pltpu.roll(t, RET_DK // 2, 1))
        return t * cos_ref[...] + partner * sin_ref[...]

    q = rope(q)
    k = rope(k) * (RET_DK ** -0.5)

    if carry:
        @pl.when(step % steps_per_seq == 0)
        def _():
            s_scr[...] = jnp.zeros_like(s_scr)
            ubuf[0:HIST, :] = jnp.zeros((HIST, CONV_CH), F32)
    else:
        s_scr[...] = jnp.zeros_like(s_scr)

    lane_c = lax.broadcasted_iota(I32, (c, QK_W), 1)
    for s in range(ns):
        r0 = s * c
        qc, kc, vc = q[r0:r0 + c], k[r0:r0 + c], v[r0:r0 + c]
        if not carry:
            for h in range(RET_HEADS):
                s_scr[h * RET_DK:(h + 1) * RET_DK, h * RET_DV:(h + 1) * RET_DV] = s0_ref[s, h]
        state = s_scr[...]
        o_cross = _mm(qc, state, _NN, precise) * qdec_ref[...]
        heads = []
        for h in range(RET_HEADS):
            qh = jnp.where((lane_c >> 6) == h, qc, 0.0)
            scores = _mm(qh, kc, _NT, precise) * intra_ref[h]
            heads.append(_mm(scores, vc[:, h * RET_DV:(h + 1) * RET_DV], _NN, precise))
        o = jnp.concatenate(heads, axis=1) + o_cross
        kv = _mm(kc * kdec_ref[...], vc, _TN, precise)
        s_new = decs_ref[...] * state + masks_ref[...] * kv
        if carry:
            s_scr[...] = s_new

            @pl.when(step % steps_per_seq == steps_per_seq - 1)
            def _():
                for h in range(RET_HEADS):
                    snew_ref[0, h] = s_new[h * RET_DK:(h + 1) * RET_DK, h * RET_DV:(h + 1) * RET_DV]
        else:
            for h in range(RET_HEADS):
                snew_ref[s, h] = s_new[h * RET_DK:(h + 1) * RET_DK, h * RET_DV:(h + 1) * RET_DV]

        normed = []
        for h in range(RET_HEADS):
            oh = o[:, h * RET_DV:(h + 1) * RET_DV]
            normed.append(oh * lax.rsqrt(jnp.mean(oh * oh, axis=-1, keepdims=True) + EPS))
        gc = gate[r0:r0 + c]
        og = jnp.concatenate(normed, axis=1) * (gc * jax.nn.sigmoid(gc))

        u = ca[r0:r0 + c] * jax.nn.sigmoid(cb[r0:r0 + c])
        if not carry:
            ubuf[0:HIST, :] = c0_ref[s]
        ubuf[HIST:HIST + c, :] = u
        off = HIST - (CONV_WIDTH - 1)
        for b in range(SUB):
            nrow = c + (2 * SUB if off + b + c - 1 >= c + SUB else SUB)
            for rb in range(0, nrow, 32):
                nr = min(32, nrow - rb)
                acc = None
                for j in range(b, CONV_WIDTH, SUB):
                    term = ubuf[rb + j - b:rb + j - b + nr, :] * cw_ref[j:j + 1, :]
                    acc = term if acc is None else acc + term
                p_scr[b, rb:rb + nr, :] = acc
        for rb in range(0, c, 32):
            acc = jnp.broadcast_to(cb_ref[...], (32, CONV_CH))
            for b in range(SUB):
                acc = acc + p_scr[b, rb + off + b:rb + off + b + 32, :]
            cv_scr[rb:rb + 32, :] = acc
        tail = ubuf[c:c + HIST, :]
        if carry:
            ubuf[0:HIST, :] = tail

            @pl.when(step % steps_per_seq == steps_per_seq - 1)
            def _():
                cnew_ref[0] = tail
        else:
            cnew_ref[s] = tail
        cv = cv_scr[0:c, :]
        mu = jnp.mean(cv, axis=-1, keepdims=True)
        xc = cv - mu
        var = jnp.mean(xc * xc, axis=-1, keepdims=True)
        cn = xc * lax.rsqrt(var + EPS) * lng_ref[...] + lnb_ref[...]
        cact = cn * jax.nn.sigmoid(cn)
        cat_scr[r0:r0 + c, 0:V_W] = og
        cat_scr[r0:r0 + c, V_W:D_MODEL] = cact

    y = _mm_w(cat_scr[...], wout_ref[...], woutlo_ref[...] if precise else None)
    _store_rt(out_ref, x + y, TM)


def _mixer_kernel(x_is_rt, precise, *refs):
    if x_is_rt:
        xp_ref = xs_ref = refs[0]
        refs = refs[1:]
    else:
        xp_ref, xs_ref = refs[0], refs[1]
        refs = refs[2:]
    if precise:
        winlo_ref, woutlo_ref = refs[0], refs[1]
        refs = refs[2:]
    else:
        winlo_ref = woutlo_ref = None
    (g_ref, win_ref, cw_ref, cb_ref, lng_ref, lnb_ref, wout_ref, masks_ref,
     cos_p, sin_p, intra_p, qdec_p, kdec_p, decs_p,
     cos_s, sin_s, intra_s, qdec_s, kdec_s, decs_s, s0_ref, c0_ref,
     out_ref, snew_p, cnew_p, snew_s, cnew_s, s_scr, ubuf, p_scr, cv_scr, cat_scr) = refs
    step = pl.program_id(0)

    @pl.when(step < N_PROMPT_STEPS)
    def _():
        _mixer_body(C_PROMPT, TM // C_PROMPT, True, SEQ // TM, x_is_rt, precise,
                    xp_ref, g_ref, win_ref, winlo_ref, cos_p, sin_p, intra_p, qdec_p, kdec_p, decs_p,
                    masks_ref, cw_ref, cb_ref, lng_ref, lnb_ref, wout_ref, woutlo_ref, None, None,
                    out_ref, snew_p, cnew_p, s_scr, ubuf, p_scr, cv_scr, cat_scr)

    @pl.when(step >= N_PROMPT_STEPS)
    def _():
        _mixer_body(DEC_SEQ, TM // DEC_SEQ, False, 1, x_is_rt, precise,
                    xs_ref, g_ref, win_ref, winlo_ref, cos_s, sin_s, intra_s, qdec_s, kdec_s, decs_s,
                    masks_ref, cw_ref, cb_ref, lng_ref, lnb_ref, wout_ref, woutlo_ref, s0_ref, c0_ref,
                    out_ref, snew_s, cnew_s, s_scr, ubuf, p_scr, cv_scr, cat_scr)


def _retention_tables(c, pos0):
    h = jnp.arange(RET_HEADS, dtype=F32)
    log_g = jnp.log1p(-jnp.exp2(-5.0 - h))
    idx = jnp.arange(c, dtype=F32)
    rel = idx[:, None] - idx[None, :]
    intra = jnp.where(rel >= 0, jnp.exp(log_g[:, None, None] * jnp.maximum(rel, 0.0)), 0.0)
    q_dec = jnp.exp(log_g[None, :] * (idx + 1.0)[:, None])
    k_dec = jnp.exp(log_g[None, :] * (c - 1.0 - idx)[:, None])
    chunk_dec = jnp.exp(log_g * c)
    qdec = jnp.repeat(q_dec, RET_DV, axis=1)
    kdec = jnp.repeat(k_dec, RET_DK, axis=1)
    row_h = jnp.arange(QK_W) // RET_DK
    col_h = jnp.arange(V_W) // RET_DV
    masks = (row_h[:, None] == col_h[None, :]).astype(F32)
    decs = masks * chunk_dec[row_h][:, None]
    return intra, qdec, kdec, decs, masks


def _rope_tables(pos):
    half = RET_DK // 2
    inv = ROPE_BASE ** (-jnp.arange(half, dtype=F32) / half)
    ang = pos.astype(F32)[:, None] * inv[None, :]
    cos, sin = jnp.cos(ang), jnp.sin(ang)
    cos_t = jnp.tile(jnp.concatenate([cos, cos], axis=1), (1, RET_HEADS))
    sin_t = jnp.tile(jnp.concatenate([-sin, sin], axis=1), (1, RET_HEADS))
    return cos_t, sin_t


def _mixer_layer(xs, g, win, cw, cb, lng, lnb, wout, s0, c0, tabs_p, tabs_s, lo_weights=()):
    precise = len(lo_weights) == 2
    lo_specs = [_const((D_MODEL, IN_COLS)), _const((D_MODEL, D_MODEL))] if precise else []
    x_is_rt = len(xs) == 1
    if x_is_rt:
        x_specs = [pl.BlockSpec((TM * SUB, LANES), lambda i: (i, 0))]
    else:
        x_specs = [pl.BlockSpec((TM, D_MODEL), lambda i: (jnp.minimum(i, N_PROMPT_STEPS - 1), 0)),
                   pl.BlockSpec((TM, D_MODEL), lambda i: (jnp.maximum(i - N_PROMPT_STEPS, 0), 0))]
    cos_p, sin_p, intra_p, qdec_p, kdec_p, decs_p, masks = tabs_p
    cos_s, sin_s, intra_s, qdec_s, kdec_s, decs_s, _ = tabs_s
    steps = SEQ // TM
    ns = TM // DEC_SEQ

    def table_specs(c):
        return [_const((RET_HEADS, c, c)), _const((c, V_W)), _const((c, QK_W)), _const((QK_W, V_W))]

    def sample_blk(i):
        return jnp.maximum(i - N_PROMPT_STEPS, 0)

    def prompt_blk(i):
        return jnp.minimum(i // steps, BATCH - 1)

    def prompt_pos(i):
        return (jnp.minimum(i, N_PROMPT_STEPS - 1) % steps, 0)

    return pl.pallas_call(
        functools.partial(_mixer_kernel, x_is_rt, precise),
        grid=(T_ALL // TM,),
        in_specs=x_specs + lo_specs + [_const((1, D_MODEL)), _const((D_MODEL, IN_COLS)),
                  _const((HIST, CONV_CH)), _const((1, CONV_CH)), _const((1, CONV_CH)), _const((1, CONV_CH)),
                  _const((D_MODEL, D_MODEL)), _const((QK_W, V_W)),
                  pl.BlockSpec((TM, QK_W), prompt_pos), pl.BlockSpec((TM, QK_W), prompt_pos)]
        + table_specs(C_PROMPT) + [_const((TM, QK_W)), _const((TM, QK_W))] + table_specs(DEC_SEQ)
        + [pl.BlockSpec((ns, RET_HEADS, RET_DK, RET_DV), lambda i: (sample_blk(i), 0, 0, 0)),
           pl.BlockSpec((ns, HIST, CONV_CH), lambda i: (sample_blk(i), 0, 0))],
        out_specs=[pl.BlockSpec((TM * SUB, LANES), lambda i: (i, 0)),
                   pl.BlockSpec((1, RET_HEADS, RET_DK, RET_DV), lambda i: (prompt_blk(i), 0, 0, 0)),
                   pl.BlockSpec((1, HIST, CONV_CH), lambda i: (prompt_blk(i), 0, 0)),
                   pl.BlockSpec((ns, RET_HEADS, RET_DK, RET_DV), lambda i: (sample_blk(i), 0, 0, 0)),
                   pl.BlockSpec((ns, HIST, CONV_CH), lambda i: (sample_blk(i), 0, 0))],
        out_shape=[jax.ShapeDtypeStruct((T_ALL * SUB, LANES), F32),
                   jax.ShapeDtypeStruct((BATCH, RET_HEADS, RET_DK, RET_DV), F32),
                   jax.ShapeDtypeStruct((BATCH, HIST, CONV_CH), F32),
                   jax.ShapeDtypeStruct((DEC_BATCH, RET_HEADS, RET_DK, RET_DV), F32),
                   jax.ShapeDtypeStruct((DEC_BATCH, HIST, CONV_CH), F32)],
        scratch_shapes=[pltpu.VMEM((QK_W, V_W), F32), pltpu.VMEM((HIST + TM, CONV_CH), F32),
                        pltpu.VMEM((SUB, TM + 2 * SUB, CONV_CH), F32),
                        pltpu.VMEM((TM, CONV_CH), F32), pltpu.VMEM((TM, D_MODEL), F32)],
        compiler_params=_params(),
        name="mixer",
    )(*xs, *lo_weights, g, win, cw, cb, lng, lnb, wout, masks, cos_p, sin_p, intra_p, qdec_p, kdec_p, decs_p,
      cos_s, sin_s, intra_s, qdec_s, kdec_s, decs_s, s0, c0)


def _rmsnorm(x, g):
    return x * lax.rsqrt(jnp.mean(x * x, axis=-1, keepdims=True) + EPS) * g


def _router_kernel(h_ref, g_ref, wr_ref, br_ref, eid_ref, wts_ref):
    xn = _rmsnorm(_load_rt(h_ref, TR), g_ref[...])
    logits = lax.dot_general(wr_ref[...], xn, _NT, preferred_element_type=F32,
                             precision=lax.Precision.HIGHEST) + br_ref[...]
    best = logits[0:1, :]
    gidx = jnp.zeros((1, TR), I32)
    for r in range(1, N_GROUPS):
        row = logits[r:r + 1, :]
        upd = row > best
        gidx = jnp.where(upd, r, gidx)
        best = jnp.where(upd, row, best)
    denom = jnp.zeros((1, TR), F32)
    for r in range(N_GROUPS):
        denom = denom + jnp.exp(logits[r:r + 1, :] - best)
    pgate = 1.0 / denom
    le = logits[EPG:2 * EPG, :]
    for gi in range(1, N_GROUPS):
        le = jnp.where(gidx == gi, logits[EPG * (gi + 1):EPG * (gi + 2), :], le)
    sub = lax.broadcasted_iota(I32, (EPG, TR), 0)
    v1 = jnp.max(le, axis=0, keepdims=True)
    i1 = jnp.min(jnp.where(le == v1, sub, EPG), axis=0, keepdims=True)
    le2 = jnp.where(sub == i1, -jnp.inf, le)
    v2 = jnp.max(le2, axis=0, keepdims=True)
    i2 = jnp.min(jnp.where(le2 == v2, sub, EPG), axis=0, keepdims=True)
    e2 = jnp.exp(v2 - v1)
    den2 = 1.0 + e2
    wa = (1.0 / den2) * pgate
    wb = (e2 / den2) * pgate
    eid_ref[0:1, :] = gidx * EPG + i1
    eid_ref[1:2, :] = gidx * EPG + i2
    wts_ref[0:1, :] = wa
    wts_ref[1:2, :] = wb


def _plan_kernel(eid_ref, upper_ref, ltri_ref, lpos_ref, ntab_ref, lbtab_ref, gdtab_ref, te_ref, nt_ref,
                 padstart_ref, padlen_ref, cnt_scr, run_scr):
    phase = pl.program_id(0)
    i = pl.program_id(1)
    rows = lax.broadcasted_iota(I32, (N_EXPERTS, TR), 0)
    oh0 = (rows == eid_ref[0:1, :]).astype(F32)
    oh1 = (rows == eid_ref[1:2, :]).astype(F32)
    tot0 = jnp.sum(oh0, axis=1, keepdims=True)
    tot1 = jnp.sum(oh1, axis=1, keepdims=True)
    eye = (lax.broadcasted_iota(I32, (N_EXPERTS, LANES), 0)
           == lax.broadcasted_iota(I32, (N_EXPERTS, LANES), 1)).astype(F32)

    def to_row(col):
        return jnp.sum(col * eye, axis=0, keepdims=True).astype(I32)

    @pl.when((phase == 0) & (i == 0))
    def _():
        cnt_scr[...] = jnp.zeros_like(cnt_scr)

    @pl.when(phase == 0)
    def _():
        cnt_scr[...] = cnt_scr[...] + (tot0 + tot1)

    @pl.when((phase == 1) & (i == 0))
    def _():
        cnt = cnt_scr[...].astype(I32)
        ntile = ((cnt + (TE - 1)) >> (TE.bit_length() - 1)).astype(F32)
        incl = jnp.dot(ltri_ref[...], ntile, preferred_element_type=F32,
                       precision=lax.Precision.HIGHEST)
        run_scr[...] = (incl - ntile) * float(TE)
        ntot = incl[N_EXPERTS - 1:N_EXPERTS, 0:1]
        j = lax.broadcasted_iota(I32, (N_EXPERTS, 256), 1).astype(F32)
        j = jnp.minimum(j, ntot - 1.0)
        te = jnp.sum((incl[:, 0:1] <= j).astype(F32), axis=0, keepdims=True)
        te_ref[...] = te.astype(I32)
        nt_ref[...] = incl[N_EXPERTS - 1:N_EXPERTS, :].astype(I32)
        cntf = cnt_scr[:, 0:1]
        padstart_ref[...] = to_row((incl[:, 0:1] - ntile[:, 0:1]) * float(TE) + cntf)
        padlen_ref[...] = to_row(ntile[:, 0:1] * float(TE) - cntf)

    @pl.when(phase == 1)
    def _():
        run = run_scr[:, 0:1]
        tot = tot0 + tot1
        lbase = jnp.dot(ltri_ref[...], jnp.broadcast_to(tot, (N_EXPERTS, LANES)), preferred_element_type=F32,
                        precision=lax.Precision.HIGHEST)[:, 0:1] - tot
        ex0 = jnp.dot(oh0.astype(BF16), upper_ref[...], preferred_element_type=F32)
        ex1 = jnp.dot(oh1.astype(BF16), upper_ref[...], preferred_element_type=F32)
        p0 = jnp.sum(oh0 * (lbase + ex0), axis=0, keepdims=True)
        p1 = jnp.sum(oh1 * (lbase + tot0 + ex1), axis=0, keepdims=True)
        lpos_ref[0:1, :] = p0.astype(I32)
        lpos_ref[1:2, :] = p1.astype(I32)
        ntab_ref[0] = to_row(tot)
        lbtab_ref[0] = to_row(lbase)
        gdtab_ref[0] = to_row(run)
        run_scr[...] = run_scr[...] + tot


def _rows(start, n):
    return pl.ds(pl.multiple_of(start * SUB, SUB), n * SUB)


def _segment_copies(src_ref, src0, dst_ref, dst0, n, sem, wait):
    def piece(off, size):
        cp = pltpu.make_async_copy(src_ref.at[_rows(src0 + off, size)], dst_ref.at[_rows(dst0 + off, size)], sem)
        if wait:
            cp.wait()
        else:
            cp.start()

    nfull = n >> (SEG.bit_length() - 1)

    def full(j, carry):
        piece(j * SEG, SEG)
        return carry

    lax.fori_loop(0, nfull, full, 0)
    rem = n - nfull * SEG
    bit = SEG // 2
    while bit >= 1:
        @pl.when((rem & bit) != 0)
        def _(bit=bit):
            piece(nfull * SEG + (rem & ~(2 * bit - 1)), bit)
        bit //= 2


def _dispatch_kernel(ntab, lbtab, gdtab, padstart, padlen, nt, lpos_ref, h_ref, g_ref, xs_ref,
                     xn_scr, ls_scr, zero_scr, sem):
    b = pl.program_id(0)
    _store_rt(xn_scr, _rmsnorm(_load_rt(h_ref, TR), g_ref[...]), TR)

    def place(t, carry):
        row = xn_scr[_rows(t, 1), :]
        ls_scr[_rows(lpos_ref[0, 0, t], 1), :] = row
        ls_scr[_rows(lpos_ref[0, 1, t], 1), :] = row
        return carry

    lax.fori_loop(0, TR, place, 0, unroll=8)

    for wait in (False, True):
        def per_expert(e, carry, wait=wait):
            _segment_copies(ls_scr, lbtab[b, e], xs_ref, gdtab[b, e], ntab[b, e], sem, wait)
            return carry

        lax.fori_loop(0, N_EXPERTS, per_expert, 0)

    @pl.when(b == pl.num_programs(0) - 1)
    def _():
        zero_scr[...] = jnp.zeros_like(zero_scr)
        for wait in (False, True):
            def per_expert(e, carry, wait=wait):
                _segment_copies(zero_scr, 0, xs_ref, padstart[e], padlen[e], sem, wait)
                return carry

            lax.fori_loop(0, N_EXPERTS, per_expert, 0)

            def per_tile(j, carry, wait=wait):
                cp = pltpu.make_async_copy(zero_scr, xs_ref.at[_rows(j * TE, TE)], sem)
                if wait:
                    cp.wait()
                else:
                    cp.start()
                return carry

            lax.fori_loop(nt[0], MAX_TILES, per_tile, 0)


def _ffn_kernel(te_ref, nt_ref, xs_ref, wg_ref, wu_ref, wd_ref, ys_ref, wg_scr, wu_scr, wd_scr):
    i = pl.program_id(0)
    first = (i == 0) | (te_ref[i] != te_ref[jnp.maximum(i - 1, 0)])

    @pl.when(first)
    def _():
        wg_scr[...] = wg_ref[0].astype(BF16)
        wu_scr[...] = wu_ref[0].astype(BF16)
        wd_scr[...] = wd_ref[0].astype(BF16)

    @pl.when(i < nt_ref[0])
    def _():
        x = _load_rt(xs_ref, TE).astype(BF16)
        a = jnp.dot(x, wg_scr[...], preferred_element_type=F32)
        b = jnp.dot(x, wu_scr[...], preferred_element_type=F32)
        hdn = (a * jax.nn.sigmoid(a)) * b
        _store_rt(ys_ref, jnp.dot(hdn.astype(BF16), wd_scr[...], preferred_element_type=F32), TE)

    @pl.when(i >= nt_ref[0])
    def _():
        ys_ref[...] = jnp.zeros_like(ys_ref)


def _combine_kernel(final, ntab, lbtab, gdtab, lpos_ref, w_ref, h_ref, gf_ref, ys_ref, *rest):
    if final:
        yp_ref, ysamp_ref, yl_scr, o_scr, sem = rest
        dst = o_scr
    else:
        out_ref, yl_scr, sem = rest
        dst = out_ref
    b = pl.program_id(0)
    for wait in (False, True):
        def per_expert(e, carry, wait=wait):
            _segment_copies(ys_ref, gdtab[b, e], yl_scr, lbtab[b, e], ntab[b, e], sem, wait)
            return carry

        lax.fori_loop(0, N_EXPERTS, per_expert, 0)

    def merge(t, carry):
        ya = yl_scr[_rows(lpos_ref[0, 0, t], 1), :]
        yb = yl_scr[_rows(lpos_ref[0, 1, t], 1), :]
        dst[_rows(t, 1), :] = h_ref[_rows(t, 1), :] + (w_ref[0, 0, t] * ya + w_ref[0, 1, t] * yb)
        return carry

    lax.fori_loop(0, TR, merge, 0, unroll=8)
    if final:
        y = _rmsnorm(_load_rt(o_scr, TR), gf_ref[...])

        @pl.when(b < T_PROMPT // TR)
        def _():
            yp_ref[...] = y

        @pl.when(b >= T_PROMPT // TR)
        def _():
            ysamp_ref[...] = y


def _moe_layer(h_all, g, wr, br, wg, wu, wd, gf, upper, ltri, final):
    n_r = T_ALL // TR
    h_blk = pl.BlockSpec((TR * SUB, LANES), lambda i, *_: (i, 0))
    eid, wts = pl.pallas_call(
        _router_kernel,
        grid=(n_r,),
        in_specs=[h_blk, _const((1, D_MODEL)), _const((40, D_MODEL)), _const((40, 1))],
        out_specs=[pl.BlockSpec((2, TR), lambda i: (0, i)), pl.BlockSpec((2, TR), lambda i: (0, i))],
        out_shape=[jax.ShapeDtypeStruct((2, T_ALL), I32), jax.ShapeDtypeStruct((2, T_ALL), F32)],
        compiler_params=_params(),
        name="router",
    )(h_all, g, wr, br)

    tab = jax.ShapeDtypeStruct((n_r, 1, LANES), I32)
    tab_blk = pl.BlockSpec((1, 1, LANES), lambda p, i: (i * p, 0, 0))
    lpos, ntab, lbtab, gdtab, te, nt, padstart, padlen = pl.pallas_call(
        _plan_kernel,
        grid=(2, n_r),
        in_specs=[pl.BlockSpec((2, TR), lambda p, i: (0, i)), _const((TR, TR)), _const((N_EXPERTS, N_EXPERTS))],
        out_specs=[pl.BlockSpec((2, TR), lambda p, i: (0, i * p)), tab_blk, tab_blk, tab_blk,
                   _const((1, 256)), _const((1, LANES)), _const((1, LANES)), _const((1, LANES))],
        out_shape=[jax.ShapeDtypeStruct((2, T_ALL), I32), tab, tab, tab, jax.ShapeDtypeStruct((1, 256), I32),
                   jax.ShapeDtypeStruct((1, LANES), I32), jax.ShapeDtypeStruct((1, LANES), I32),
                   jax.ShapeDtypeStruct((1, LANES), I32)],
        scratch_shapes=[pltpu.VMEM((N_EXPERTS, LANES), F32), pltpu.VMEM((N_EXPERTS, LANES), F32)],
        compiler_params=_params(2),
        name="plan",
    )(eid, upper, ltri)

    seg_tabs = (ntab.reshape(n_r, LANES), lbtab.reshape(n_r, LANES), gdtab.reshape(n_r, LANES))
    lpos3 = lpos.reshape(2, n_r, TR).transpose(1, 0, 2)
    wts3 = wts.reshape(2, n_r, TR).transpose(1, 0, 2)
    smem_blk = pl.BlockSpec((1, 2, TR), lambda i, *_: (i, 0, 0), memory_space=pltpu.SMEM)

    xs = pl.pallas_call(
        _dispatch_kernel,
        grid_spec=pltpu.PrefetchScalarGridSpec(
            num_scalar_prefetch=6,
            grid=(n_r,),
            in_specs=[smem_blk, h_blk, _const((1, D_MODEL))],
            out_specs=pl.BlockSpec(memory_space=pl.ANY),
            scratch_shapes=[pltpu.VMEM((TR * SUB, LANES), F32), pltpu.VMEM((2 * TR * SUB, LANES), F32),
                            pltpu.VMEM((TE * SUB, LANES), F32), pltpu.SemaphoreType.DMA(())]),
        out_shape=jax.ShapeDtypeStruct((P_ROWS * SUB, LANES), F32),
        compiler_params=_params(),
        name="dispatch",
    )(*seg_tabs, padstart.reshape(LANES), padlen.reshape(LANES), nt[0, 0:1], lpos3, h_all, g)

    def row_map(i, te_ref, nt_ref):
        return (jnp.minimum(i, nt_ref[0] - 1), 0)

    ys = pl.pallas_call(
        _ffn_kernel,
        grid_spec=pltpu.PrefetchScalarGridSpec(
            num_scalar_prefetch=2,
            grid=(MAX_TILES,),
            in_specs=[pl.BlockSpec((TE * SUB, LANES), row_map),
                      pl.BlockSpec((1, D_MODEL, EXPERT_FF), lambda i, te_ref, nt_ref: (te_ref[i], 0, 0)),
                      pl.BlockSpec((1, D_MODEL, EXPERT_FF), lambda i, te_ref, nt_ref: (te_ref[i], 0, 0)),
                      pl.BlockSpec((1, EXPERT_FF, D_MODEL), lambda i, te_ref, nt_ref: (te_ref[i], 0, 0))],
            out_specs=pl.BlockSpec((TE * SUB, LANES), lambda i, te_ref, nt_ref: (i, 0)),
            scratch_shapes=[pltpu.VMEM((D_MODEL, EXPERT_FF), BF16), pltpu.VMEM((D_MODEL, EXPERT_FF), BF16),
                            pltpu.VMEM((EXPERT_FF, D_MODEL), BF16)]),
        out_shape=jax.ShapeDtypeStruct((P_ROWS * SUB, LANES), F32),
        compiler_params=_params(),
        name="expert_ffn",
    )(te.reshape(256), nt[0, 0:1], xs, wg, wu, wd)

    n_p = T_PROMPT // TR
    if final:
        out_specs = [pl.BlockSpec((TR, D_MODEL), lambda i, *_: (jnp.minimum(i, n_p - 1), 0)),
                     pl.BlockSpec((TR, D_MODEL), lambda i, *_: (jnp.maximum(i - n_p, 0), 0))]
        out_shape = [jax.ShapeDtypeStruct((T_PROMPT, D_MODEL), F32), jax.ShapeDtypeStruct((T_SAMPLE, D_MODEL), F32)]
        scratch = [pltpu.VMEM((2 * TR * SUB, LANES), F32), pltpu.VMEM((TR * SUB, LANES), F32),
                   pltpu.SemaphoreType.DMA(())]
    else:
        out_specs = h_blk
        out_shape = jax.ShapeDtypeStruct((T_ALL * SUB, LANES), F32)
        scratch = [pltpu.VMEM((2 * TR * SUB, LANES), F32), pltpu.SemaphoreType.DMA(())]
    return pl.pallas_call(
        functools.partial(_combine_kernel, final),
        grid_spec=pltpu.PrefetchScalarGridSpec(
            num_scalar_prefetch=3,
            grid=(n_r,),
            in_specs=[smem_blk, smem_blk, h_blk, _const((1, D_MODEL)), pl.BlockSpec(memory_space=pl.ANY)],
            out_specs=out_specs,
            scratch_shapes=scratch),
        out_shape=out_shape,
        compiler_params=_params(),
        name="combine",
    )(*seg_tabs, lpos3, wts3, h_all, gf, ys)


def kernel(x_prompt, x_sample, state_ret, cache_conv, norm_mix_g, w_in, conv_w, conv_b, conv_ln_g, conv_ln_b,
           w_out, norm_ffn_g, router_group_w, router_group_b, router_expert_w, router_expert_b, expert_w_gate,
           expert_w_up, expert_w_down, norm_final_g):
    h = (x_prompt.reshape(T_PROMPT, D_MODEL), x_sample.reshape(T_SAMPLE, D_MODEL))

    cos_p, sin_p = _rope_tables(jnp.arange(SEQ))
    cos_s, sin_s = _rope_tables(PAST_LEN + jnp.arange(DEC_SEQ))
    reps = TM // DEC_SEQ
    tabs_p = (cos_p, sin_p) + _retention_tables(C_PROMPT, 0)
    tabs_s = (jnp.tile(cos_s, (reps, 1)), jnp.tile(sin_s, (reps, 1))) + _retention_tables(DEC_SEQ, PAST_LEN)

    upper = (jnp.arange(TR)[:, None] < jnp.arange(TR)[None, :]).astype(BF16)
    ltri = (jnp.arange(N_EXPERTS)[:, None] >= jnp.arange(N_EXPERTS)[None, :]).astype(F32)
    pad_hist = HIST - (CONV_WIDTH - 1)
    gf = norm_final_g.reshape(1, D_MODEL)

    ret_p, conv_p, ret_s, conv_s = [], [], [], []
    for l in range(DEPTH):
        cw = jnp.pad(conv_w[l], ((0, HIST - CONV_WIDTH), (0, 0)))
        c0 = jnp.pad(cache_conv[l], ((0, 0), (pad_hist, 0), (0, 0)))
        if l == 0:
            (win_hi, win_lo), (wout_hi, wout_lo) = _split(w_in[l]), _split(w_out[l])
            lo_weights = (win_lo, wout_lo)
        else:
            win_hi, wout_hi, lo_weights = w_in[l].astype(BF16), w_out[l].astype(BF16), ()
        hm, sp, cp, ss, cs = _mixer_layer(
            h, norm_mix_g[l].reshape(1, D_MODEL), win_hi, cw, conv_b[l].reshape(1, CONV_CH),
            conv_ln_g[l].reshape(1, CONV_CH), conv_ln_b[l].reshape(1, CONV_CH), wout_hi,
            state_ret[l], c0, tabs_p, tabs_s, lo_weights)
        ret_p.append(sp)
        conv_p.append(cp[:, pad_hist:])
        ret_s.append(ss)
        conv_s.append(cs[:, pad_hist:])

        wr = jnp.zeros((40, D_MODEL), F32)
        wr = wr.at[0:N_GROUPS].set(router_group_w[l].T)
        wr = wr.at[EPG:].set(router_expert_w[l].transpose(0, 2, 1).reshape(N_EXPERTS, D_MODEL))
        br = jnp.zeros((40, 1), F32)
        br = br.at[0:N_GROUPS, 0].set(router_group_b[l])
        br = br.at[EPG:, 0].set(router_expert_b[l].reshape(N_EXPERTS))
        h = _moe_layer(hm, norm_ffn_g[l].reshape(1, D_MODEL), wr, br, expert_w_gate[l], expert_w_up[l],
                       expert_w_down[l], gf, upper, ltri, final=(l == DEPTH - 1))
        if l < DEPTH - 1:
            h = (h,)

    y_prompt = h[0].reshape(BATCH, SEQ, D_MODEL)
    y_sample = h[1].reshape(DEC_BATCH, DEC_SEQ, D_MODEL)
    return (y_prompt, y_sample, jnp.stack(ret_p), jnp.stack(conv_p), jnp.stack(ret_s), jnp.stack(conv_s))
```

```python
import functools

import jax
import jax.numpy as jnp
from jax import lax
from jax.experimental import pallas as pl
from jax.experimental.pallas import tpu as pltpu

F32 = jnp.float32
BF16 = jnp.bfloat16
I32 = jnp.int32

D_MODEL = 1024
DEPTH = 2
BATCH, SEQ = 2, 8192
DEC_BATCH, DEC_SEQ = 16, 64
PAST_LEN = 1024
RET_HEADS, RET_DK, RET_DV = 4, 64, 128
QK_W = RET_HEADS * RET_DK
V_W = RET_HEADS * RET_DV
CONV_CH = 512
CONV_WIDTH = 31
HIST = 32
IN_COLS = 2 * QK_W + 2 * V_W + 2 * CONV_CH
N_GROUPS, EPG = 4, 8
N_EXPERTS = N_GROUPS * EPG
EXPERT_FF = 512
ROPE_BASE = 10000.0
EPS = 1e-6

T_PROMPT = BATCH * SEQ
T_SAMPLE = DEC_BATCH * DEC_SEQ
T_ALL = T_PROMPT + T_SAMPLE

SUB, LANES = 8, 128
TM = 256
C_PROMPT = 256
TR = 512
TE = 512
SEG = 32
N_SLOTS = 2 * T_ALL
MAX_TILES = N_SLOTS // TE + N_EXPERTS
P_ROWS = MAX_TILES * TE
VMEM_LIMIT = 48 * 1024 * 1024
N_PROMPT_STEPS = T_PROMPT // TM

SPLIT_FACTOR = float(2 ** 16 + 1)
_NN = (((1,), (0,)), ((), ()))
_NT = (((1,), (1,)), ((), ()))
_TN = (((0,), (0,)), ((), ()))


def _params(n_axes=1):
    return pltpu.CompilerParams(dimension_semantics=("arbitrary",) * n_axes,
                                vmem_limit_bytes=VMEM_LIMIT)


def _const(shape):
    nd = len(shape)
    return pl.BlockSpec(shape, lambda *_: (0,) * nd)


def _load_rt(ref, rows):
    return jnp.concatenate([ref[pl.ds(s, rows, stride=SUB), :] for s in range(SUB)], axis=1)


def _store_rt(ref, val, rows):
    for s in range(SUB):
        ref[pl.ds(s, rows, stride=SUB), :] = val[:, s * LANES:(s + 1) * LANES]


def _split(a):
    t = a * SPLIT_FACTOR
    hi = t - (t - a)
    return hi.astype(BF16), (a - hi).astype(BF16)


def _mm(a, b, dims, precise):
    if not precise:
        return lax.dot_general(a.astype(BF16), b.astype(BF16), dims, preferred_element_type=F32)
    ah, al = _split(a)
    bh, bl = _split(b)
    return (lax.dot_general(ah, bh, dims, preferred_element_type=F32)
            + (lax.dot_general(al, bh, dims, preferred_element_type=F32)
               + lax.dot_general(ah, bl, dims, preferred_element_type=F32)))


def _mm_w(a, whi, wlo):
    if wlo is None:
        return jnp.dot(a.astype(BF16), whi, preferred_element_type=F32)
    ah, al = _split(a)
    return (jnp.dot(ah, whi, preferred_element_type=F32)
            + (jnp.dot(al, whi, preferred_element_type=F32) + jnp.dot(ah, wlo, preferred_element_type=F32)))


def _mixer_body(c, ns, carry, steps_per_seq, x_is_rt, precise,
                x_ref, g_ref, win_ref, winlo_ref, cos_ref, sin_ref, intra_ref, qdec_ref, kdec_ref, decs_ref,
                masks_ref, cw_ref, cb_ref, lng_ref, lnb_ref, wout_ref, woutlo_ref, s0_ref, c0_ref,
                out_ref, snew_ref, cnew_ref, s_scr, ubuf, p_scr, cv_scr, cat_scr):
    step = pl.program_id(0)
    x = _load_rt(x_ref, TM) if x_is_rt else x_ref[...]
    xn = x * lax.rsqrt(jnp.mean(x * x, axis=-1, keepdims=True) + EPS) * g_ref[...]
    xn_parts = _split(xn) if precise else (xn.astype(BF16), None)

    def proj(lo, hi):
        out = jnp.dot(xn_parts[0], win_ref[:, lo:hi], preferred_element_type=F32)
        if precise:
            out = out + (jnp.dot(xn_parts[1], win_ref[:, lo:hi], preferred_element_type=F32)
                         + jnp.dot(xn_parts[0], winlo_ref[:, lo:hi], preferred_element_type=F32))
        return out

    q = proj(0, QK_W)
    k = proj(QK_W, 2 * QK_W)
    v = proj(2 * QK_W, 2 * QK_W + V_W)
    gate = proj(2 * QK_W + V_W, 2 * QK_W + 2 * V_W)
    ca = proj(2 * QK_W + 2 * V_W, 2 * QK_W + 2 * V_W + CONV_CH)
    cb = proj(2 * QK_W + 2 * V_W + CONV_CH, IN_COLS)

    lane = lax.broadcasted_iota(I32, (TM, QK_W), 1)
    first_half = (lane & (RET_DK // 2)) == 0

    def rope(t):
        partner = jnp.where(first_half, pltpu.roll(t, QK_W - RET_DK // 2, 1), pltpu.roll(t, RET_DK // 2, 1))
        return t * cos_ref[...] + partner * sin_ref[...]

    q = rope(q)
    k = rope(k) * (RET_DK ** -0.5)

    if carry:
        @pl.when(step % steps_per_seq == 0)
        def _():
            s_scr[...] = jnp.zeros_like(s_scr)
            ubuf[0:HIST, :] = jnp.zeros((HIST, CONV_CH), F32)
    else:
        s_scr[...] = jnp.zeros_like(s_scr)

    lane_c = lax.broadcasted_iota(I32, (c, QK_W), 1)
    for s in range(ns):
        r0 = s * c
        qc, kc, vc = q[r0:r0 + c], k[r0:r0 + c], v[r0:r0 + c]
        if not carry:
            for h in range(RET_HEADS):
                s_scr[h * RET_DK:(h + 1) * RET_DK, h * RET_DV:(h + 1) * RET_DV] = s0_ref[s, h]
        state = s_scr[...]
        o_cross = _mm(qc, state, _NN, precise) * qdec_ref[...]
        heads = []
        for h in range(RET_HEADS):
            qh = jnp.where((lane_c >> 6) == h, qc, 0.0)
            scores = _mm(qh, kc, _NT, precise) * intra_ref[h]
            heads.append(_mm(scores, vc[:, h * RET_DV:(h + 1) * RET_DV], _NN, precise))
        o = jnp.concatenate(heads, axis=1) + o_cross
        kv = _mm(kc * kdec_ref[...], vc, _TN, precise)
        s_new = decs_ref[...] * state + masks_ref[...] * kv
        if carry:
            s_scr[...] = s_new

            @pl.when(step % steps_per_seq == steps_per_seq - 1)
            def _():
                for h in range(RET_HEADS):
                    snew_ref[0, h] = s_new[h * RET_DK:(h + 1) * RET_DK, h * RET_DV:(h + 1) * RET_DV]
        else:
            for h in range(RET_HEADS):
                snew_ref[s, h] = s_new[h * RET_DK:(h + 1) * RET_DK, h * RET_DV:(h + 1) * RET_DV]

        normed = []
        for h in range(RET_HEADS):
            oh = o[:, h * RET_DV:(h + 1) * RET_DV]
            normed.append(oh * lax.rsqrt(jnp.mean(oh * oh, axis=-1, keepdims=True) + EPS))
        gc = gate[r0:r0 + c]
        og = jnp.concatenate(normed, axis=1) * (gc * jax.nn.sigmoid(gc))

        u = ca[r0:r0 + c] * jax.nn.sigmoid(cb[r0:r0 + c])
        if not carry:
            ubuf[0:HIST, :] = c0_ref[s]
        ubuf[HIST:HIST + c, :] = u
        off = HIST - (CONV_WIDTH - 1)
        for b in range(SUB):
            nrow = c + (2 * SUB if off + b + c - 1 >= c + SUB else SUB)
            for rb in range(0, nrow, 32):
                nr = min(32, nrow - rb)
                acc = None
                for j in range(b, CONV_WIDTH, SUB):
                    term = ubuf[rb + j - b:rb + j - b + nr, :] * cw_ref[j:j + 1, :]
                    acc = term if acc is None else acc + term
                p_scr[b, rb:rb + nr, :] = acc
        for rb in range(0, c, 32):
            acc = jnp.broadcast_to(cb_ref[...], (32, CONV_CH))
            for b in range(SUB):
                acc = acc + p_scr[b, rb + off + b:rb + off + b + 32, :]
            cv_scr[rb:rb + 32, :] = acc
        tail = ubuf[c:c + HIST, :]
        if carry:
            ubuf[0:HIST, :] = tail

            @pl.when(step % steps_per_seq == steps_per_seq - 1)
            def _():
                cnew_ref[0] = tail
        else:
            cnew_ref[s] = tail
        cv = cv_scr[0:c, :]
        mu = jnp.mean(cv, axis=-1, keepdims=True)
        xc = cv - mu
        var = jnp.mean(xc * xc, axis=-1, keepdims=True)
        cn = xc * lax.rsqrt(var + EPS) * lng_ref[...] + lnb_ref[...]
        cact = cn * jax.nn.sigmoid(cn)
        cat_scr[r0:r0 + c, 0:V_W] = og
        cat_scr[r0:r0 + c, V_W:D_MODEL] = cact

    y = _mm_w(cat_scr[...], wout_ref[...], woutlo_ref[...] if precise else None)
    _store_rt(out_ref, x + y, TM)


def _mixer_kernel(x_is_rt, precise, *refs):
    if x_is_rt:
        xp_ref = xs_ref = refs[0]
        refs = refs[1:]
    else:
        xp_ref, xs_ref = refs[0], refs[1]
        refs = refs[2:]
    if precise:
        winlo_ref, woutlo_ref = refs[0], refs[1]
        refs = refs[2:]
    else:
        winlo_ref = woutlo_ref = None
    (g_ref, win_ref, cw_ref, cb_ref, lng_ref, lnb_ref, wout_ref, masks_ref,
     cos_p, sin_p, intra_p, qdec_p, kdec_p, decs_p,
     cos_s, sin_s, intra_s, qdec_s, kdec_s, decs_s, s0_ref, c0_ref,
     out_ref, snew_p, cnew_p, snew_s, cnew_s, s_scr, ubuf, p_scr, cv_scr, cat_scr) = refs
    step = pl.program_id(0)

    @pl.when(step < N_PROMPT_STEPS)
    def _():
        _mixer_body(C_PROMPT, TM // C_PROMPT, True, SEQ // TM, x_is_rt, precise,
                    xp_ref, g_ref, win_ref, winlo_ref, cos_p, sin_p, intra_p, qdec_p, kdec_p, decs_p,
                    masks_ref, cw_ref, cb_ref, lng_ref, lnb_ref, wout_ref, woutlo_ref, None, None,
                    out_ref, snew_p, cnew_p, s_scr, ubuf, p_scr, cv_scr, cat_scr)

    @pl.when(step >= N_PROMPT_STEPS)
    def _():
        _mixer_body(DEC_SEQ, TM // DEC_SEQ, False, 1, x_is_rt, precise,
                    xs_ref, g_ref, win_ref, winlo_ref, cos_s, sin_s, intra_s, qdec_s, kdec_s, decs_s,
                    masks_ref, cw_ref, cb_ref, lng_ref, lnb_ref, wout_ref, woutlo_ref, s0_ref, c0_ref,
                    out_ref, snew_s, cnew_s, s_scr, ubuf, p_scr, cv_scr, cat_scr)


def _retention_tables(c, pos0):
    h = jnp.arange(RET_HEADS, dtype=F32)
    log_g = jnp.log1p(-jnp.exp2(-5.0 - h))
    idx = jnp.arange(c, dtype=F32)
    rel = idx[:, None] - idx[None, :]
    intra = jnp.where(rel >= 0, jnp.exp(log_g[:, None, None] * jnp.maximum(rel, 0.0)), 0.0)
    q_dec = jnp.exp(log_g[None, :] * (idx + 1.0)[:, None])
    k_dec = jnp.exp(log_g[None, :] * (c - 1.0 - idx)[:, None])
    chunk_dec = jnp.exp(log_g * c)
    qdec = jnp.repeat(q_dec, RET_DV, axis=1)
    kdec = jnp.repeat(k_dec, RET_DK, axis=1)
    row_h = jnp.arange(QK_W) // RET_DK
    col_h = jnp.arange(V_W) // RET_DV
    masks = (row_h[:, None] == col_h[None, :]).astype(F32)
    decs = masks * chunk_dec[row_h][:, None]
    return intra, qdec, kdec, decs, masks


def _rope_tables(pos):
    half = RET_DK // 2
    inv = ROPE_BASE ** (-jnp.arange(half, dtype=F32) / half)
    ang = pos.astype(F32)[:, None] * inv[None, :]
    cos, sin = jnp.cos(ang), jnp.sin(ang)
    cos_t = jnp.tile(jnp.concatenate([cos, cos], axis=1), (1, RET_HEADS))
    sin_t = jnp.tile(jnp.concatenate([-sin, sin], axis=1), (1, RET_HEADS))
    return cos_t, sin_t


def _mixer_layer(xs, g, win, cw, cb, lng, lnb, wout, s0, c0, tabs_p, tabs_s, lo_weights=()):
    precise = len(lo_weights) == 2
    lo_specs = [_const((D_MODEL, IN_COLS)), _const((D_MODEL, D_MODEL))] if precise else []
    x_is_rt = len(xs) == 1
    if x_is_rt:
        x_specs = [pl.BlockSpec((TM * SUB, LANES), lambda i: (i, 0))]
    else:
        x_specs = [pl.BlockSpec((TM, D_MODEL), lambda i: (jnp.minimum(i, N_PROMPT_STEPS - 1), 0)),
                   pl.BlockSpec((TM, D_MODEL), lambda i: (jnp.maximum(i - N_PROMPT_STEPS, 0), 0))]
    cos_p, sin_p, intra_p, qdec_p, kdec_p, decs_p, masks = tabs_p
    cos_s, sin_s, intra_s, qdec_s, kdec_s, decs_s, _ = tabs_s
    steps = SEQ // TM
    ns = TM // DEC_SEQ

    def table_specs(c):
        return [_const((RET_HEADS, c, c)), _const((c, V_W)), _const((c, QK_W)), _const((QK_W, V_W))]

    def sample_blk(i):
        return jnp.maximum(i - N_PROMPT_STEPS, 0)

    def prompt_blk(i):
        return jnp.minimum(i // steps, BATCH - 1)

    def prompt_pos(i):
        return (jnp.minimum(i, N_PROMPT_STEPS - 1) % steps, 0)

    return pl.pallas_call(
        functools.partial(_mixer_kernel, x_is_rt, precise),
        grid=(T_ALL // TM,),
        in_specs=x_specs + lo_specs + [_const((1, D_MODEL)), _const((D_MODEL, IN_COLS)),
                  _const((HIST, CONV_CH)), _const((1, CONV_CH)), _const((1, CONV_CH)), _const((1, CONV_CH)),
                  _const((D_MODEL, D_MODEL)), _const((QK_W, V_W)),
                  pl.BlockSpec((TM, QK_W), prompt_pos), pl.BlockSpec((TM, QK_W), prompt_pos)]
        + table_specs(C_PROMPT) + [_const((TM, QK_W)), _const((TM, QK_W))] + table_specs(DEC_SEQ)
        + [pl.BlockSpec((ns, RET_HEADS, RET_DK, RET_DV), lambda i: (sample_blk(i), 0, 0, 0)),
           pl.BlockSpec((ns, HIST, CONV_CH), lambda i: (sample_blk(i), 0, 0))],
        out_specs=[pl.BlockSpec((TM * SUB, LANES), lambda i: (i, 0)),
                   pl.BlockSpec((1, RET_HEADS, RET_DK, RET_DV), lambda i: (prompt_blk(i), 0, 0, 0)),
                   pl.BlockSpec((1, HIST, CONV_CH), lambda i: (prompt_blk(i), 0, 0)),
                   pl.BlockSpec((ns, RET_HEADS, RET_DK, RET_DV), lambda i: (sample_blk(i), 0, 0, 0)),
                   pl.BlockSpec((ns, HIST, CONV_CH), lambda i: (sample_blk(i), 0, 0))],
        out_shape=[jax.ShapeDtypeStruct((T_ALL * SUB, LANES), F32),
                   jax.ShapeDtypeStruct((BATCH, RET_HEADS, RET_DK, RET_DV), F32),
                   jax.ShapeDtypeStruct((BATCH, HIST, CONV_CH), F32),
                   jax.ShapeDtypeStruct((DEC_BATCH, RET_HEADS, RET_DK, RET_DV), F32),
                   jax.ShapeDtypeStruct((DEC_BATCH, HIST, CONV_CH), F32)],
        scratch_shapes=[pltpu.VMEM((QK_W, V_W), F32), pltpu.VMEM((HIST + TM, CONV_CH), F32),
                        pltpu.VMEM((SUB, TM + 2 * SUB, CONV_CH), F32),
                        pltpu.VMEM((TM, CONV_CH), F32), pltpu.VMEM((TM, D_MODEL), F32)],
        compiler_params=_params(),
        name="mixer",
    )(*xs, *lo_weights, g, win, cw, cb, lng, lnb, wout, masks, cos_p, sin_p, intra_p, qdec_p, kdec_p, decs_p,
      cos_s, sin_s, intra_s, qdec_s, kdec_s, decs_s, s0, c0)


def _rmsnorm(x, g):
    return x * lax.rsqrt(jnp.mean(x * x, axis=-1, keepdims=True) + EPS) * g


def _router_kernel(h_ref, g_ref, wr_ref, br_ref, eid_ref, wts_ref):
    xn = _rmsnorm(_load_rt(h_ref, TR), g_ref[...])
    logits = lax.dot_general(wr_ref[...], xn, _NT, preferred_element_type=F32,
                             precision=lax.Precision.HIGHEST) + br_ref[...]
    best = logits[0:1, :]
    gidx = jnp.zeros((1, TR), I32)
    for r in range(1, N_GROUPS):
        row = logits[r:r + 1, :]
        upd = row > best
        gidx = jnp.where(upd, r, gidx)
        best = jnp.where(upd, row, best)
    denom = jnp.zeros((1, TR), F32)
    for r in range(N_GROUPS):
        denom = denom + jnp.exp(logits[r:r + 1, :] - best)
    pgate = 1.0 / denom
    le = logits[EPG:2 * EPG, :]
    for gi in range(1, N_GROUPS):
        le = jnp.where(gidx == gi, logits[EPG * (gi + 1):EPG * (gi + 2), :], le)
    sub = lax.broadcasted_iota(I32, (EPG, TR), 0)
    v1 = jnp.max(le, axis=0, keepdims=True)
    i1 = jnp.min(jnp.where(le == v1, sub, EPG), axis=0, keepdims=True)
    le2 = jnp.where(sub == i1, -jnp.inf, le)
    v2 = jnp.max(le2, axis=0, keepdims=True)
    i2 = jnp.min(jnp.where(le2 == v2, sub, EPG), axis=0, keepdims=True)
    e2 = jnp.exp(v2 - v1)
    den2 = 1.0 + e2
    wa = (1.0 / den2) * pgate
    wb = (e2 / den2) * pgate
    eid_ref[0:1, :] = gidx * EPG + i1
    eid_ref[1:2, :] = gidx * EPG + i2
    wts_ref[0:1, :] = wa
    wts_ref[1:2, :] = wb


def _plan_kernel(eid_ref, upper_ref, ltri_ref, lpos_ref, ntab_ref, lbtab_ref, gdtab_ref, te_ref, nt_ref,
                 padstart_ref, padlen_ref, cnt_scr, run_scr):
    phase = pl.program_id(0)
    i = pl.program_id(1)
    rows = lax.broadcasted_iota(I32, (N_EXPERTS, TR), 0)
    oh0 = (rows == eid_ref[0:1, :]).astype(F32)
    oh1 = (rows == eid_ref[1:2, :]).astype(F32)
    tot0 = jnp.sum(oh0, axis=1, keepdims=True)
    tot1 = jnp.sum(oh1, axis=1, keepdims=True)
    eye = (lax.broadcasted_iota(I32, (N_EXPERTS, LANES), 0)
           == lax.broadcasted_iota(I32, (N_EXPERTS, LANES), 1)).astype(F32)

    def to_row(col):
        return jnp.sum(col * eye, axis=0, keepdims=True).astype(I32)

    @pl.when((phase == 0) & (i == 0))
    def _():
        cnt_scr[...] = jnp.zeros_like(cnt_scr)

    @pl.when(phase == 0)
    def _():
        cnt_scr[...] = cnt_scr[...] + (tot0 + tot1)

    @pl.when((phase == 1) & (i == 0))
    def _():
        cnt = cnt_scr[...].astype(I32)
        ntile = ((cnt + (TE - 1)) >> (TE.bit_length() - 1)).astype(F32)
        incl = jnp.dot(ltri_ref[...], ntile, preferred_element_type=F32,
                       precision=lax.Precision.HIGHEST)
        run_scr[...] = (incl - ntile) * float(TE)
        ntot = incl[N_EXPERTS - 1:N_EXPERTS, 0:1]
        j = lax.broadcasted_iota(I32, (N_EXPERTS, 256), 1).astype(F32)
        j = jnp.minimum(j, ntot - 1.0)
        te = jnp.sum((incl[:, 0:1] <= j).astype(F32), axis=0, keepdims=True)
        te_ref[...] = te.astype(I32)
        nt_ref[...] = incl[N_EXPERTS - 1:N_EXPERTS, :].astype(I32)
        cntf = cnt_scr[:, 0:1]
        padstart_ref[...] = to_row((incl[:, 0:1] - ntile[:, 0:1]) * float(TE) + cntf)
        padlen_ref[...] = to_row(ntile[:, 0:1] * float(TE) - cntf)

    @pl.when(phase == 1)
    def _():
        run = run_scr[:, 0:1]
        tot = tot0 + tot1
        lbase = jnp.dot(ltri_ref[...], jnp.broadcast_to(tot, (N_EXPERTS, LANES)), preferred_element_type=F32,
                        precision=lax.Precision.HIGHEST)[:, 0:1] - tot
        ex0 = jnp.dot(oh0.astype(BF16), upper_ref[...], preferred_element_type=F32)
        ex1 = jnp.dot(oh1.astype(BF16), upper_ref[...], preferred_element_type=F32)
        p0 = jnp.sum(oh0 * (lbase + ex0), axis=0, keepdims=True)
        p1 = jnp.sum(oh1 * (lbase + tot0 + ex1), axis=0, keepdims=True)
        lpos_ref[0:1, :] = p0.astype(I32)
        lpos_ref[1:2, :] = p1.astype(I32)
        ntab_ref[0] = to_row(tot)
        lbtab_ref[0] = to_row(lbase)
        gdtab_ref[0] = to_row(run)
        run_scr[...] = run_scr[...] + tot


def _rows(start, n):
    return pl.ds(pl.multiple_of(start * SUB, SUB), n * SUB)


def _segment_copies(src_ref, src0, dst_ref, dst0, n, sem, wait):
    def piece(off, size):
        cp = pltpu.make_async_copy(src_ref.at[_rows(src0 + off, size)], dst_ref.at[_rows(dst0 + off, size)], sem)
        if wait:
            cp.wait()
        else:
            cp.start()

    nfull = n >> (SEG.bit_length() - 1)

    def full(j, carry):
        piece(j * SEG, SEG)
        return carry

    lax.fori_loop(0, nfull, full, 0)
    rem = n - nfull * SEG
    bit = SEG // 2
    while bit >= 1:
        @pl.when((rem & bit) != 0)
        def _(bit=bit):
            piece(nfull * SEG + (rem & ~(2 * bit - 1)), bit)
        bit //= 2


def _dispatch_kernel(ntab, lbtab, gdtab, padstart, padlen, nt, lpos_ref, h_ref, g_ref, xs_ref,
                     xn_scr, ls_scr, zero_scr, sems):
    b = pl.program_id(0)
    last = pl.num_programs(0) - 1
    slot = b % 2
    _store_rt(xn_scr, _rmsnorm(_load_rt(h_ref, TR), g_ref[...]), TR)

    def place(t, carry):
        row = xn_scr[_rows(t, 1), :]
        ls_scr[slot, _rows(lpos_ref[0, 0, t], 1), :] = row
        ls_scr[slot, _rows(lpos_ref[0, 1, t], 1), :] = row
        return carry

    lax.fori_loop(0, TR, place, 0, unroll=8)

    def segments(blk, sl, wait):
        def per_expert(e, carry):
            _segment_copies(ls_scr.at[sl], lbtab[blk, e], xs_ref, gdtab[blk, e], ntab[blk, e], sems.at[sl], wait)
            return carry

        lax.fori_loop(0, N_EXPERTS, per_expert, 0)

    segments(b, slot, False)

    @pl.when(b > 0)
    def _():
        segments(b - 1, 1 - slot, True)

    @pl.when(b == last)
    def _():
        segments(b, slot, True)
        zero_scr[...] = jnp.zeros_like(zero_scr)
        for wait in (False, True):
            def per_expert(e, carry, wait=wait):
                _segment_copies(zero_scr, 0, xs_ref, padstart[e], padlen[e], sems.at[0], wait)
                return carry

            lax.fori_loop(0, N_EXPERTS, per_expert, 0)

            def per_tile(j, carry, wait=wait):
                cp = pltpu.make_async_copy(zero_scr, xs_ref.at[_rows(j * TE, TE)], sems.at[0])
                if wait:
                    cp.wait()
                else:
                    cp.start()
                return carry

            lax.fori_loop(nt[0], MAX_TILES, per_tile, 0)


def _ffn_kernel(te_ref, nt_ref, xs_ref, wg_ref, wu_ref, wd_ref, ys_ref, wg_scr, wu_scr, wd_scr):
    i = pl.program_id(0)
    first = (i == 0) | (te_ref[i] != te_ref[jnp.maximum(i - 1, 0)])

    @pl.when(first)
    def _():
        wg_scr[...] = wg_ref[0, 0].astype(BF16)
        wu_scr[...] = wu_ref[0, 0].astype(BF16)
        wd_scr[...] = wd_ref[0, 0].astype(BF16)

    @pl.when(i < nt_ref[0])
    def _():
        x = _load_rt(xs_ref, TE).astype(BF16)
        a = jnp.dot(x, wg_scr[...], preferred_element_type=F32)
        b = jnp.dot(x, wu_scr[...], preferred_element_type=F32)
        hdn = (a * jax.nn.sigmoid(a)) * b
        _store_rt(ys_ref, jnp.dot(hdn.astype(BF16), wd_scr[...], preferred_element_type=F32), TE)

    @pl.when(i >= nt_ref[0])
    def _():
        ys_ref[...] = jnp.zeros_like(ys_ref)


def _combine_kernel(final, ntab, lbtab, gdtab, lpos_ref, w_ref, h_ref, gf_ref, ys_ref, *rest):
    if final:
        yp_ref, ysamp_ref, yl_scr, o_scr, sems = rest
        dst = o_scr
    else:
        out_ref, yl_scr, sems = rest
        dst = out_ref
    b = pl.program_id(0)
    slot = b % 2

    def segments(blk, sl, wait):
        def per_expert(e, carry):
            _segment_copies(ys_ref, gdtab[blk, e], yl_scr.at[sl], lbtab[blk, e], ntab[blk, e], sems.at[sl], wait)
            return carry

        lax.fori_loop(0, N_EXPERTS, per_expert, 0)

    @pl.when(b == 0)
    def _():
        segments(0, 0, False)

    @pl.when(b + 1 < pl.num_programs(0))
    def _():
        segments(b + 1, 1 - slot, False)

    segments(b, slot, True)

    def merge(t, carry):
        ya = yl_scr[slot, _rows(lpos_ref[0, 0, t], 1), :]
        yb = yl_scr[slot, _rows(lpos_ref[0, 1, t], 1), :]
        dst[_rows(t, 1), :] = h_ref[_rows(t, 1), :] + (w_ref[0, 0, t] * ya + w_ref[0, 1, t] * yb)
        return carry

    lax.fori_loop(0, TR, merge, 0, unroll=8)
    if final:
        y = _rmsnorm(_load_rt(o_scr, TR), gf_ref[...])

        @pl.when(b < T_PROMPT // TR)
        def _():
            yp_ref[...] = y

        @pl.when(b >= T_PROMPT // TR)
        def _():
            ysamp_ref[...] = y


def _moe_layer(h_all, g, wr, br, layer, wg, wu, wd, gf, upper, ltri, final):
    n_r = T_ALL // TR
    h_blk = pl.BlockSpec((TR * SUB, LANES), lambda i, *_: (i, 0))
    eid, wts = pl.pallas_call(
        _router_kernel,
        grid=(n_r,),
        in_specs=[h_blk, _const((1, D_MODEL)), _const((40, D_MODEL)), _const((40, 1))],
        out_specs=[pl.BlockSpec((2, TR), lambda i: (0, i)), pl.BlockSpec((2, TR), lambda i: (0, i))],
        out_shape=[jax.ShapeDtypeStruct((2, T_ALL), I32), jax.ShapeDtypeStruct((2, T_ALL), F32)],
        compiler_params=_params(),
        name="router",
    )(h_all, g, wr, br)

    tab = jax.ShapeDtypeStruct((n_r, 1, LANES), I32)
    tab_blk = pl.BlockSpec((1, 1, LANES), lambda p, i: (i * p, 0, 0))
    lpos, ntab, lbtab, gdtab, te, nt, padstart, padlen = pl.pallas_call(
        _plan_kernel,
        grid=(2, n_r),
        in_specs=[pl.BlockSpec((2, TR), lambda p, i: (0, i)), _const((TR, TR)), _const((N_EXPERTS, N_EXPERTS))],
        out_specs=[pl.BlockSpec((2, TR), lambda p, i: (0, i * p)), tab_blk, tab_blk, tab_blk,
                   _const((1, 256)), _const((1, LANES)), _const((1, LANES)), _const((1, LANES))],
        out_shape=[jax.ShapeDtypeStruct((2, T_ALL), I32), tab, tab, tab, jax.ShapeDtypeStruct((1, 256), I32),
                   jax.ShapeDtypeStruct((1, LANES), I32), jax.ShapeDtypeStruct((1, LANES), I32),
                   jax.ShapeDtypeStruct((1, LANES), I32)],
        scratch_shapes=[pltpu.VMEM((N_EXPERTS, LANES), F32), pltpu.VMEM((N_EXPERTS, LANES), F32)],
        compiler_params=_params(2),
        name="plan",
    )(eid, upper, ltri)

    seg_tabs = (ntab.reshape(n_r, LANES), lbtab.reshape(n_r, LANES), gdtab.reshape(n_r, LANES))
    lpos3 = lpos.reshape(2, n_r, TR).transpose(1, 0, 2)
    wts3 = wts.reshape(2, n_r, TR).transpose(1, 0, 2)
    smem_blk = pl.BlockSpec((1, 2, TR), lambda i, *_: (i, 0, 0), memory_space=pltpu.SMEM)

    xs = pl.pallas_call(
        _dispatch_kernel,
        grid_spec=pltpu.PrefetchScalarGridSpec(
            num_scalar_prefetch=6,
            grid=(n_r,),
            in_specs=[smem_blk, h_blk, _const((1, D_MODEL))],
            out_specs=pl.BlockSpec(memory_space=pl.ANY),
            scratch_shapes=[pltpu.VMEM((TR * SUB, LANES), F32), pltpu.VMEM((2, 2 * TR * SUB, LANES), F32),
                            pltpu.VMEM((TE * SUB, LANES), F32), pltpu.SemaphoreType.DMA((2,))]),
        out_shape=jax.ShapeDtypeStruct((P_ROWS * SUB, LANES), F32),
        compiler_params=_params(),
        name="dispatch",
    )(*seg_tabs, padstart.reshape(LANES), padlen.reshape(LANES), nt[0, 0:1], lpos3, h_all, g)

    def row_map(i, te_ref, nt_ref):
        return (jnp.minimum(i, nt_ref[0] - 1), 0)

    ys = pl.pallas_call(
        _ffn_kernel,
        grid_spec=pltpu.PrefetchScalarGridSpec(
            num_scalar_prefetch=2,
            grid=(MAX_TILES,),
            in_specs=[pl.BlockSpec((TE * SUB, LANES), row_map),
                      pl.BlockSpec((1, 1, D_MODEL, EXPERT_FF), lambda i, te_ref, nt_ref: (layer, te_ref[i], 0, 0)),
                      pl.BlockSpec((1, 1, D_MODEL, EXPERT_FF), lambda i, te_ref, nt_ref: (layer, te_ref[i], 0, 0)),
                      pl.BlockSpec((1, 1, EXPERT_FF, D_MODEL), lambda i, te_ref, nt_ref: (layer, te_ref[i], 0, 0))],
            out_specs=pl.BlockSpec((TE * SUB, LANES), lambda i, te_ref, nt_ref: (i, 0)),
            scratch_shapes=[pltpu.VMEM((D_MODEL, EXPERT_FF), BF16), pltpu.VMEM((D_MODEL, EXPERT_FF), BF16),
                            pltpu.VMEM((EXPERT_FF, D_MODEL), BF16)]),
        out_shape=jax.ShapeDtypeStruct((P_ROWS * SUB, LANES), F32),
        compiler_params=_params(),
        name="expert_ffn",
    )(te.reshape(256), nt[0, 0:1], xs, wg, wu, wd)

    n_p = T_PROMPT // TR
    if final:
        out_specs = [pl.BlockSpec((TR, D_MODEL), lambda i, *_: (jnp.minimum(i, n_p - 1), 0)),
                     pl.BlockSpec((TR, D_MODEL), lambda i, *_: (jnp.maximum(i - n_p, 0), 0))]
        out_shape = [jax.ShapeDtypeStruct((T_PROMPT, D_MODEL), F32), jax.ShapeDtypeStruct((T_SAMPLE, D_MODEL), F32)]
        scratch = [pltpu.VMEM((2, 2 * TR * SUB, LANES), F32), pltpu.VMEM((TR * SUB, LANES), F32),
                   pltpu.SemaphoreType.DMA((2,))]
    else:
        out_specs = h_blk
        out_shape = jax.ShapeDtypeStruct((T_ALL * SUB, LANES), F32)
        scratch = [pltpu.VMEM((2, 2 * TR * SUB, LANES), F32), pltpu.SemaphoreType.DMA((2,))]
    return pl.pallas_call(
        functools.partial(_combine_kernel, final),
        grid_spec=pltpu.PrefetchScalarGridSpec(
            num_scalar_prefetch=3,
            grid=(n_r,),
            in_specs=[smem_blk, smem_blk, h_blk, _const((1, D_MODEL)), pl.BlockSpec(memory_space=pl.ANY)],
            out_specs=out_specs,
            scratch_shapes=scratch),
        out_shape=out_shape,
        compiler_params=_params(),
        name="combine",
    )(*seg_tabs, lpos3, wts3, h_all, gf, ys)


def kernel(x_prompt, x_sample, state_ret, cache_conv, norm_mix_g, w_in, conv_w, conv_b, conv_ln_g, conv_ln_b,
           w_out, norm_ffn_g, router_group_w, router_group_b, router_expert_w, router_expert_b, expert_w_gate,
           expert_w_up, expert_w_down, norm_final_g):
    h = (x_prompt.reshape(T_PROMPT, D_MODEL), x_sample.reshape(T_SAMPLE, D_MODEL))

    cos_p, sin_p = _rope_tables(jnp.arange(SEQ))
    cos_s, sin_s = _rope_tables(PAST_LEN + jnp.arange(DEC_SEQ))
    reps = TM // DEC_SEQ
    tabs_p = (cos_p, sin_p) + _retention_tables(C_PROMPT, 0)
    tabs_s = (jnp.tile(cos_s, (reps, 1)), jnp.tile(sin_s, (reps, 1))) + _retention_tables(DEC_SEQ, PAST_LEN)

    upper = (jnp.arange(TR)[:, None] < jnp.arange(TR)[None, :]).astype(BF16)
    ltri = (jnp.arange(N_EXPERTS)[:, None] >= jnp.arange(N_EXPERTS)[None, :]).astype(F32)
    pad_hist = HIST - (CONV_WIDTH - 1)
    gf = norm_final_g.reshape(1, D_MODEL)

    ret_p, conv_p, ret_s, conv_s = [], [], [], []
    for l in range(DEPTH):
        cw = jnp.pad(conv_w[l], ((0, HIST - CONV_WIDTH), (0, 0)))
        c0 = jnp.pad(cache_conv[l], ((0, 0), (pad_hist, 0), (0, 0)))
        if l == 0:
            (win_hi, win_lo), (wout_hi, wout_lo) = _split(w_in[l]), _split(w_out[l])
            lo_weights = (win_lo, wout_lo)
        else:
            win_hi, wout_hi, lo_weights = w_in[l].astype(BF16), w_out[l].astype(BF16), ()
        hm, sp, cp, ss, cs = _mixer_layer(
            h, norm_mix_g[l].reshape(1, D_MODEL), win_hi, cw, conv_b[l].reshape(1, CONV_CH),
            conv_ln_g[l].reshape(1, CONV_CH), conv_ln_b[l].reshape(1, CONV_CH), wout_hi,
            state_ret[l], c0, tabs_p, tabs_s, lo_weights)
        ret_p.append(sp)
        conv_p.append(cp[:, pad_hist:])
        ret_s.append(ss)
        conv_s.append(cs[:, pad_hist:])

        wr = jnp.zeros((40, D_MODEL), F32)
        wr = wr.at[0:N_GROUPS].set(router_group_w[l].T)
        wr = wr.at[EPG:].set(router_expert_w[l].transpose(0, 2, 1).reshape(N_EXPERTS, D_MODEL))
        br = jnp.zeros((40, 1), F32)
        br = br.at[0:N_GROUPS, 0].set(router_group_b[l])
        br = br.at[EPG:, 0].set(router_expert_b[l].reshape(N_EXPERTS))
        h = _moe_layer(hm, norm_ffn_g[l].reshape(1, D_MODEL), wr, br, l, expert_w_gate, expert_w_up,
                       expert_w_down, gf, upper, ltri, final=(l == DEPTH - 1))
        if l < DEPTH - 1:
            h = (h,)

    y_prompt = h[0].reshape(BATCH, SEQ, D_MODEL)
    y_sample = h[1].reshape(DEC_BATCH, DEC_SEQ, D_MODEL)
    return (y_prompt, y_sample, jnp.stack(ret_p), jnp.stack(conv_p), jnp.stack(ret_s), jnp.stack(conv_s))
```

```python
import functools

import jax
import jax.numpy as jnp
from jax import lax
from jax.experimental import pallas as pl
from jax.experimental.pallas import tpu as pltpu

F32 = jnp.float32
BF16 = jnp.bfloat16
I32 = jnp.int32

D_MODEL = 1024
DEPTH = 2
BATCH, SEQ = 2, 8192
DEC_BATCH, DEC_SEQ = 16, 64
PAST_LEN = 1024
RET_HEADS, RET_DK, RET_DV = 4, 64, 128
QK_W = RET_HEADS * RET_DK
V_W = RET_HEADS * RET_DV
CONV_CH = 512
CONV_WIDTH = 31
HIST = 32
IN_COLS = 2 * QK_W + 2 * V_W + 2 * CONV_CH
N_GROUPS, EPG = 4, 8
N_EXPERTS = N_GROUPS * EPG
EXPERT_FF = 512
ROPE_BASE = 10000.0
EPS = 1e-6

T_PROMPT = BATCH * SEQ
T_SAMPLE = DEC_BATCH * DEC_SEQ
T_ALL = T_PROMPT + T_SAMPLE

SUB, LANES = 8, 128
TM = 512
C_PROMPT = 256
TR = 1024
TE = 512
SEG = 32
N_SLOTS = 2 * T_ALL
MAX_TILES = N_SLOTS // TE + N_EXPERTS
P_ROWS = MAX_TILES * TE
VMEM_LIMIT = 56 * 1024 * 1024
N_PROMPT_STEPS = T_PROMPT // TM

SPLIT_FACTOR = float(2 ** 16 + 1)
_NN = (((1,), (0,)), ((), ()))
_NT = (((1,), (1,)), ((), ()))
_TN = (((0,), (0,)), ((), ()))


def _params(n_axes=1):
    return pltpu.CompilerParams(dimension_semantics=("arbitrary",) * n_axes,
                                vmem_limit_bytes=VMEM_LIMIT)


def _const(shape):
    nd = len(shape)
    return pl.BlockSpec(shape, lambda *_: (0,) * nd)


def _resident(shape):
    nd = len(shape)
    return pl.BlockSpec(shape, lambda *_: (0,) * nd, pipeline_mode=pl.Buffered(1))


def _load_rt(ref, rows):
    return jnp.concatenate([ref[pl.ds(s, rows, stride=SUB), :] for s in range(SUB)], axis=1)


def _store_rt(ref, val, rows):
    for s in range(SUB):
        ref[pl.ds(s, rows, stride=SUB), :] = val[:, s * LANES:(s + 1) * LANES]


def _split(a):
    t = a * SPLIT_FACTOR
    hi = t - (t - a)
    return hi.astype(BF16), (a - hi).astype(BF16)


def _mm(a, b, dims, precise):
    if not precise:
        return lax.dot_general(a.astype(BF16), b.astype(BF16), dims, preferred_element_type=F32)
    ah, al = _split(a)
    bh, bl = _split(b)
    return (lax.dot_general(ah, bh, dims, preferred_element_type=F32)
            + (lax.dot_general(al, bh, dims, preferred_element_type=F32)
               + lax.dot_general(ah, bl, dims, preferred_element_type=F32)))


def _mm_w(a, whi, wlo):
    if wlo is None:
        return jnp.dot(a.astype(BF16), whi, preferred_element_type=F32)
    ah, al = _split(a)
    return (jnp.dot(ah, whi, preferred_element_type=F32)
            + (jnp.dot(al, whi, preferred_element_type=F32) + jnp.dot(ah, wlo, preferred_element_type=F32)))


def _mixer_body(c, ns, carry, steps_per_seq, x_is_rt, precise,
                x_ref, g_ref, win_ref, winlo_ref, cos_ref, sin_ref, intra_ref, qdec_ref, kdec_ref, decs_ref,
                masks_ref, cw_ref, cb_ref, lng_ref, lnb_ref, wout_ref, woutlo_ref, s0_ref, c0_ref,
                out_ref, snew_ref, cnew_ref, s_scr, ubuf, p_scr, cv_scr, cat_scr):
    step = pl.program_id(0)
    x = _load_rt(x_ref, TM) if x_is_rt else x_ref[...]
    xn = x * lax.rsqrt(jnp.mean(x * x, axis=-1, keepdims=True) + EPS) * g_ref[...]
    xn_parts = _split(xn) if precise else (xn.astype(BF16), None)

    def proj(lo, hi):
        out = jnp.dot(xn_parts[0], win_ref[:, lo:hi], preferred_element_type=F32)
        if precise:
            out = out + (jnp.dot(xn_parts[1], win_ref[:, lo:hi], preferred_element_type=F32)
                         + jnp.dot(xn_parts[0], winlo_ref[:, lo:hi], preferred_element_type=F32))
        return out

    q = proj(0, QK_W)
    k = proj(QK_W, 2 * QK_W)
    v = proj(2 * QK_W, 2 * QK_W + V_W)
    gate = proj(2 * QK_W + V_W, 2 * QK_W + 2 * V_W)
    ca = proj(2 * QK_W + 2 * V_W, 2 * QK_W + 2 * V_W + CONV_CH)
    cb = proj(2 * QK_W + 2 * V_W + CONV_CH, IN_COLS)

    lane = lax.broadcasted_iota(I32, (TM, QK_W), 1)
    first_half = (lane & (RET_DK // 2)) == 0

    def rope(t):
        partner = jnp.where(first_half, pltpu.roll(t, QK_W - RET_DK // 2, 1), pltpu.roll(t, RET_DK // 2, 1))
        return t * cos_ref[...] + partner * sin_ref[...]

    q = rope(q)
    k = rope(k) * (RET_DK ** -0.5)

    if carry:
        @pl.when(step % steps_per_seq == 0)
        def _():
            s_scr[...] = jnp.zeros_like(s_scr)
            ubuf[0:HIST, :] = jnp.zeros((HIST, CONV_CH), F32)
    else:
        s_scr[...] = jnp.zeros_like(s_scr)

    lane_c = lax.broadcasted_iota(I32, (c, QK_W), 1)
    for s in range(ns):
        r0 = s * c
        qc, kc, vc = q[r0:r0 + c], k[r0:r0 + c], v[r0:r0 + c]
        if not carry:
            for h in range(RET_HEADS):
                s_scr[h * RET_DK:(h + 1) * RET_DK, h * RET_DV:(h + 1) * RET_DV] = s0_ref[s, h]
        state = s_scr[...]
        o_cross = _mm(qc, state, _NN, precise) * qdec_ref[...]
        heads = []
        for h in range(RET_HEADS):
            qh = jnp.where((lane_c >> 6) == h, qc, 0.0)
            scores = _mm(qh, kc, _NT, precise) * intra_ref[h]
            heads.append(_mm(scores, vc[:, h * RET_DV:(h + 1) * RET_DV], _NN, precise))
        o = jnp.concatenate(heads, axis=1) + o_cross
        kv = _mm(kc * kdec_ref[...], vc, _TN, precise)
        s_new = decs_ref[...] * state + masks_ref[...] * kv
        if carry:
            s_scr[...] = s_new
            if s == ns - 1:
                @pl.when(step % steps_per_seq == steps_per_seq - 1)
                def _():
                    for h in range(RET_HEADS):
                        snew_ref[0, h] = s_new[h * RET_DK:(h + 1) * RET_DK, h * RET_DV:(h + 1) * RET_DV]
        else:
            for h in range(RET_HEADS):
                snew_ref[s, h] = s_new[h * RET_DK:(h + 1) * RET_DK, h * RET_DV:(h + 1) * RET_DV]

        normed = []
        for h in range(RET_HEADS):
            oh = o[:, h * RET_DV:(h + 1) * RET_DV]
            normed.append(oh * lax.rsqrt(jnp.mean(oh * oh, axis=-1, keepdims=True) + EPS))
        gc = gate[r0:r0 + c]
        og = jnp.concatenate(normed, axis=1) * (gc * jax.nn.sigmoid(gc))

        u = ca[r0:r0 + c] * jax.nn.sigmoid(cb[r0:r0 + c])
        if not carry:
            ubuf[0:HIST, :] = c0_ref[s]
        ubuf[HIST:HIST + c, :] = u
        off = HIST - (CONV_WIDTH - 1)
        for b in range(SUB):
            nrow = c + (2 * SUB if off + b + c - 1 >= c + SUB else SUB)
            for rb in range(0, nrow, 32):
                nr = min(32, nrow - rb)
                acc = None
                for j in range(b, CONV_WIDTH, SUB):
                    wj = jnp.concatenate([cw_ref[j * SUB:(j + 1) * SUB, :]] * (nr // SUB), axis=0)
                    term = ubuf[rb + j - b:rb + j - b + nr, :] * wj
                    acc = term if acc is None else acc + term
                p_scr[b, rb:rb + nr, :] = acc
        for rb in range(0, c, 32):
            acc = jnp.broadcast_to(cb_ref[...], (32, CONV_CH))
            for b in range(SUB):
                acc = acc + p_scr[b, rb + off + b:rb + off + b + 32, :]
            cv_scr[rb:rb + 32, :] = acc
        tail = ubuf[c:c + HIST, :]
        if carry:
            ubuf[0:HIST, :] = tail
            if s == ns - 1:
                @pl.when(step % steps_per_seq == steps_per_seq - 1)
                def _():
                    cnew_ref[0] = tail
        else:
            cnew_ref[s] = tail
        cv = cv_scr[0:c, :]
        mu = jnp.mean(cv, axis=-1, keepdims=True)
        xc = cv - mu
        var = jnp.mean(xc * xc, axis=-1, keepdims=True)
        cn = xc * lax.rsqrt(var + EPS) * lng_ref[...] + lnb_ref[...]
        cact = cn * jax.nn.sigmoid(cn)
        cat_scr[r0:r0 + c, 0:V_W] = og
        cat_scr[r0:r0 + c, V_W:D_MODEL] = cact

    y = _mm_w(cat_scr[...], wout_ref[...], woutlo_ref[...] if precise else None)
    _store_rt(out_ref, x + y, TM)


def _mixer_kernel(x_is_rt, precise, *refs):
    if x_is_rt:
        xp_ref = xs_ref = refs[0]
        refs = refs[1:]
    else:
        xp_ref, xs_ref = refs[0], refs[1]
        refs = refs[2:]
    if precise:
        winlo_ref, woutlo_ref = refs[0], refs[1]
        refs = refs[2:]
    else:
        winlo_ref = woutlo_ref = None
    (g_ref, win_ref, cw_ref, cb_ref, lng_ref, lnb_ref, wout_ref, masks_ref,
     cos_p, sin_p, intra_p, qdec_p, kdec_p, decs_p,
     cos_s, sin_s, intra_s, qdec_s, kdec_s, decs_s, s0_ref, c0_ref,
     out_ref, snew_p, cnew_p, snew_s, cnew_s, s_scr, ubuf, p_scr, cv_scr, cat_scr) = refs
    step = pl.program_id(0)

    @pl.when(step < N_PROMPT_STEPS)
    def _():
        _mixer_body(C_PROMPT, TM // C_PROMPT, True, SEQ // TM, x_is_rt, precise,
                    xp_ref, g_ref, win_ref, winlo_ref, cos_p, sin_p, intra_p, qdec_p, kdec_p, decs_p,
                    masks_ref, cw_ref, cb_ref, lng_ref, lnb_ref, wout_ref, woutlo_ref, None, None,
                    out_ref, snew_p, cnew_p, s_scr, ubuf, p_scr, cv_scr, cat_scr)

    @pl.when(step >= N_PROMPT_STEPS)
    def _():
        _mixer_body(DEC_SEQ, TM // DEC_SEQ, False, 1, x_is_rt, precise,
                    xs_ref, g_ref, win_ref, winlo_ref, cos_s, sin_s, intra_s, qdec_s, kdec_s, decs_s,
                    masks_ref, cw_ref, cb_ref, lng_ref, lnb_ref, wout_ref, woutlo_ref, s0_ref, c0_ref,
                    out_ref, snew_s, cnew_s, s_scr, ubuf, p_scr, cv_scr, cat_scr)


def _retention_tables(c, pos0):
    h = jnp.arange(RET_HEADS, dtype=F32)
    log_g = jnp.log1p(-jnp.exp2(-5.0 - h))
    idx = jnp.arange(c, dtype=F32)
    rel = idx[:, None] - idx[None, :]
    intra = jnp.where(rel >= 0, jnp.exp(log_g[:, None, None] * jnp.maximum(rel, 0.0)), 0.0)
    q_dec = jnp.exp(log_g[None, :] * (idx + 1.0)[:, None])
    k_dec = jnp.exp(log_g[None, :] * (c - 1.0 - idx)[:, None])
    chunk_dec = jnp.exp(log_g * c)
    qdec = jnp.repeat(q_dec, RET_DV, axis=1)
    kdec = jnp.repeat(k_dec, RET_DK, axis=1)
    row_h = jnp.arange(QK_W) // RET_DK
    col_h = jnp.arange(V_W) // RET_DV
    masks = (row_h[:, None] == col_h[None, :]).astype(F32)
    decs = masks * chunk_dec[row_h][:, None]
    return intra, qdec, kdec, decs, masks


def _rope_tables(pos):
    half = RET_DK // 2
    inv = ROPE_BASE ** (-jnp.arange(half, dtype=F32) / half)
    ang = pos.astype(F32)[:, None] * inv[None, :]
    cos, sin = jnp.cos(ang), jnp.sin(ang)
    cos_t = jnp.tile(jnp.concatenate([cos, cos], axis=1), (1, RET_HEADS))
    sin_t = jnp.tile(jnp.concatenate([-sin, sin], axis=1), (1, RET_HEADS))
    return cos_t, sin_t


def _mixer_layer(xs, g, win, cw, cb, lng, lnb, wout, s0, c0, tabs_p, tabs_s, lo_weights=()):
    precise = len(lo_weights) == 2
    lo_specs = [_resident((D_MODEL, IN_COLS)), _resident((D_MODEL, D_MODEL))] if precise else []
    x_is_rt = len(xs) == 1
    if x_is_rt:
        x_specs = [pl.BlockSpec((TM * SUB, LANES), lambda i: (i, 0))]
    else:
        x_specs = [pl.BlockSpec((TM, D_MODEL), lambda i: (jnp.minimum(i, N_PROMPT_STEPS - 1), 0)),
                   pl.BlockSpec((TM, D_MODEL), lambda i: (jnp.maximum(i - N_PROMPT_STEPS, 0), 0))]
    cos_p, sin_p, intra_p, qdec_p, kdec_p, decs_p, masks = tabs_p
    cos_s, sin_s, intra_s, qdec_s, kdec_s, decs_s, _ = tabs_s
    steps = SEQ // TM
    ns = TM // DEC_SEQ

    def table_specs(c):
        return [_const((RET_HEADS, c, c)), _const((c, V_W)), _const((c, QK_W)), _const((QK_W, V_W))]

    def sample_blk(i):
        return jnp.maximum(i - N_PROMPT_STEPS, 0)

    def prompt_blk(i):
        return jnp.minimum(i // steps, BATCH - 1)

    def prompt_pos(i):
        return (jnp.minimum(i, N_PROMPT_STEPS - 1) % steps, 0)

    return pl.pallas_call(
        functools.partial(_mixer_kernel, x_is_rt, precise),
        grid=(T_ALL // TM,),
        in_specs=x_specs + lo_specs + [_const((1, D_MODEL)), _resident((D_MODEL, IN_COLS)),
                  _const((HIST * SUB, CONV_CH)), _const((1, CONV_CH)), _const((1, CONV_CH)), _const((1, CONV_CH)),
                  _resident((D_MODEL, D_MODEL)), _const((QK_W, V_W)),
                  pl.BlockSpec((TM, QK_W), prompt_pos), pl.BlockSpec((TM, QK_W), prompt_pos)]
        + table_specs(C_PROMPT) + [_const((TM, QK_W)), _const((TM, QK_W))] + table_specs(DEC_SEQ)
        + [pl.BlockSpec((ns, RET_HEADS, RET_DK, RET_DV), lambda i: (sample_blk(i), 0, 0, 0)),
           pl.BlockSpec((ns, HIST, CONV_CH), lambda i: (sample_blk(i), 0, 0))],
        out_specs=[pl.BlockSpec((TM * SUB, LANES), lambda i: (i, 0)),
                   pl.BlockSpec((1, RET_HEADS, RET_DK, RET_DV), lambda i: (prompt_blk(i), 0, 0, 0)),
                   pl.BlockSpec((1, HIST, CONV_CH), lambda i: (prompt_blk(i), 0, 0)),
                   pl.BlockSpec((ns, RET_HEADS, RET_DK, RET_DV), lambda i: (sample_blk(i), 0, 0, 0)),
                   pl.BlockSpec((ns, HIST, CONV_CH), lambda i: (sample_blk(i), 0, 0))],
        out_shape=[jax.ShapeDtypeStruct((T_ALL * SUB, LANES), F32),
                   jax.ShapeDtypeStruct((BATCH, RET_HEADS, RET_DK, RET_DV), F32),
                   jax.ShapeDtypeStruct((BATCH, HIST, CONV_CH), F32),
                   jax.ShapeDtypeStruct((DEC_BATCH, RET_HEADS, RET_DK, RET_DV), F32),
                   jax.ShapeDtypeStruct((DEC_BATCH, HIST, CONV_CH), F32)],
        scratch_shapes=[pltpu.VMEM((QK_W, V_W), F32), pltpu.VMEM((HIST + C_PROMPT, CONV_CH), F32),
                        pltpu.VMEM((SUB, C_PROMPT + 2 * SUB, CONV_CH), F32),
                        pltpu.VMEM((C_PROMPT, CONV_CH), F32), pltpu.VMEM((TM, D_MODEL), F32)],
        compiler_params=_params(),
        name="mixer",
    )(*xs, *lo_weights, g, win, cw, cb, lng, lnb, wout, masks, cos_p, sin_p, intra_p, qdec_p, kdec_p, decs_p,
      cos_s, sin_s, intra_s, qdec_s, kdec_s, decs_s, s0, c0)


def _rmsnorm(x, g):
    return x * lax.rsqrt(jnp.mean(x * x, axis=-1, keepdims=True) + EPS) * g


def _router_kernel(h_ref, g_ref, wr_ref, br_ref, eid_ref, wts_ref):
    xn = _rmsnorm(_load_rt(h_ref, TR), g_ref[...])
    logits = (jnp.dot(xn, wr_ref[...], preferred_element_type=F32, precision=lax.Precision.HIGHEST)
              + br_ref[...]).T
    best = logits[0:1, :]
    gidx = jnp.zeros((1, TR), I32)
    for r in range(1, N_GROUPS):
        row = logits[r:r + 1, :]
        upd = row > best
        gidx = jnp.where(upd, r, gidx)
        best = jnp.where(upd, row, best)
    denom = jnp.zeros((1, TR), F32)
    for r in range(N_GROUPS):
        denom = denom + jnp.exp(logits[r:r + 1, :] - best)
    pgate = 1.0 / denom
    le = logits[EPG:2 * EPG, :]
    for gi in range(1, N_GROUPS):
        le = jnp.where(gidx == gi, logits[EPG * (gi + 1):EPG * (gi + 2), :], le)
    sub = lax.broadcasted_iota(I32, (EPG, TR), 0)
    v1 = jnp.max(le, axis=0, keepdims=True)
    i1 = jnp.min(jnp.where(le == v1, sub, EPG), axis=0, keepdims=True)
    le2 = jnp.where(sub == i1, -jnp.inf, le)
    v2 = jnp.max(le2, axis=0, keepdims=True)
    i2 = jnp.min(jnp.where(le2 == v2, sub, EPG), axis=0, keepdims=True)
    e2 = jnp.exp(v2 - v1)
    den2 = 1.0 + e2
    wa = (1.0 / den2) * pgate
    wb = (e2 / den2) * pgate
    eid_ref[0:1, :] = gidx * EPG + i1
    eid_ref[1:2, :] = gidx * EPG + i2
    wts_ref[0:1, :] = wa
    wts_ref[1:2, :] = wb


def _plan_kernel(eid_ref, upper_ref, ltri_ref, lpos_ref, ntab_ref, lbtab_ref, gdtab_ref, te_ref, nt_ref,
                 padstart_ref, padlen_ref, cnt_scr, run_scr):
    phase = pl.program_id(0)
    i = pl.program_id(1)
    rows = lax.broadcasted_iota(I32, (N_EXPERTS, TR), 0)
    oh0 = (rows == eid_ref[0:1, :]).astype(F32)
    oh1 = (rows == eid_ref[1:2, :]).astype(F32)
    tot0 = jnp.sum(oh0, axis=1, keepdims=True)
    tot1 = jnp.sum(oh1, axis=1, keepdims=True)
    eye = (lax.broadcasted_iota(I32, (N_EXPERTS, LANES), 0)
           == lax.broadcasted_iota(I32, (N_EXPERTS, LANES), 1)).astype(F32)

    def to_row(col):
        return jnp.sum(col * eye, axis=0, keepdims=True).astype(I32)

    @pl.when((phase == 0) & (i == 0))
    def _():
        cnt_scr[...] = jnp.zeros_like(cnt_scr)

    @pl.when(phase == 0)
    def _():
        cnt_scr[...] = cnt_scr[...] + (tot0 + tot1)

    @pl.when((phase == 1) & (i == 0))
    def _():
        cnt = cnt_scr[...].astype(I32)
        ntile = ((cnt + (TE - 1)) >> (TE.bit_length() - 1)).astype(F32)
        incl = jnp.dot(ltri_ref[...], ntile, preferred_element_type=F32,
                       precision=lax.Precision.HIGHEST)
        run_scr[...] = (incl - ntile) * float(TE)
        ntot = incl[N_EXPERTS - 1:N_EXPERTS, 0:1]
        j = lax.broadcasted_iota(I32, (N_EXPERTS, 256), 1).astype(F32)
        j = jnp.minimum(j, ntot - 1.0)
        te = jnp.sum((incl[:, 0:1] <= j).astype(F32), axis=0, keepdims=True)
        te_ref[...] = te.astype(I32)
        nt_ref[...] = incl[N_EXPERTS - 1:N_EXPERTS, :].astype(I32)
        cntf = cnt_scr[:, 0:1]
        padstart_ref[...] = to_row((incl[:, 0:1] - ntile[:, 0:1]) * float(TE) + cntf)
        padlen_ref[...] = to_row(ntile[:, 0:1] * float(TE) - cntf)

    @pl.when(phase == 1)
    def _():
        run = run_scr[:, 0:1]
        tot = tot0 + tot1
        lbase = jnp.dot(ltri_ref[...], jnp.broadcast_to(tot, (N_EXPERTS, LANES)), preferred_element_type=F32,
                        precision=lax.Precision.HIGHEST)[:, 0:1] - tot
        ex0 = jnp.dot(oh0.astype(BF16), upper_ref[...], preferred_element_type=F32)
        ex1 = jnp.dot(oh1.astype(BF16), upper_ref[...], preferred_element_type=F32)
        p0 = jnp.sum(oh0 * (lbase + ex0), axis=0, keepdims=True)
        p1 = jnp.sum(oh1 * (lbase + tot0 + ex1), axis=0, keepdims=True)
        lpos_ref[0:1, :] = p0.astype(I32)
        lpos_ref[1:2, :] = p1.astype(I32)
        ntab_ref[0] = to_row(tot)
        lbtab_ref[0] = to_row(lbase)
        gdtab_ref[0] = to_row(run)
        run_scr[...] = run_scr[...] + tot


def _rows(start, n):
    return pl.ds(pl.multiple_of(start * SUB, SUB), n * SUB)


def _segment_copies(src_ref, src0, dst_ref, dst0, n, sem, wait):
    def piece(off, size):
        cp = pltpu.make_async_copy(src_ref.at[_rows(src0 + off, size)], dst_ref.at[_rows(dst0 + off, size)], sem)
        if wait:
            cp.wait()
        else:
            cp.start()

    nfull = n >> (SEG.bit_length() - 1)

    def full(j, carry):
        piece(j * SEG, SEG)
        return carry

    lax.fori_loop(0, nfull, full, 0)
    rem = n - nfull * SEG
    bit = SEG // 2
    while bit >= 1:
        @pl.when((rem & bit) != 0)
        def _(bit=bit):
            piece(nfull * SEG + (rem & ~(2 * bit - 1)), bit)
        bit //= 2


def _dispatch_kernel(ntab, lbtab, gdtab, padstart, padlen, nt, lpos_ref, h_ref, g_ref, xs_ref,
                     xn_scr, ls_scr, zero_scr, sems):
    b = pl.program_id(0)
    last = pl.num_programs(0) - 1
    slot = b % 2
    _store_rt(xn_scr, _rmsnorm(_load_rt(h_ref, TR), g_ref[...]), TR)

    def place(t, carry):
        row = xn_scr[_rows(t, 1), :]
        ls_scr[slot, _rows(lpos_ref[0, 0, t], 1), :] = row
        ls_scr[slot, _rows(lpos_ref[0, 1, t], 1), :] = row
        return carry

    lax.fori_loop(0, TR, place, 0, unroll=8)

    def per_expert(e, carry):
        _segment_copies(ls_scr.at[slot], lbtab[b, e], xs_ref, gdtab[b, e], ntab[b, e], sems.at[slot], False)
        return carry

    lax.fori_loop(0, N_EXPERTS, per_expert, 0)

    def wait_block(sl):
        pltpu.make_async_copy(ls_scr.at[sl], xs_ref.at[_rows(0, 2 * TR)], sems.at[sl]).wait()

    @pl.when(b > 0)
    def _():
        wait_block(1 - slot)

    @pl.when(b == last)
    def _():
        wait_block(slot)
        zero_scr[...] = jnp.zeros_like(zero_scr)
        for wait in (False, True):
            def per_expert(e, carry, wait=wait):
                _segment_copies(zero_scr, 0, xs_ref, padstart[e], padlen[e], sems.at[0], wait)
                return carry

            lax.fori_loop(0, N_EXPERTS, per_expert, 0)

            def per_tile(j, carry, wait=wait):
                cp = pltpu.make_async_copy(zero_scr, xs_ref.at[_rows(j * TE, TE)], sems.at[0])
                if wait:
                    cp.wait()
                else:
                    cp.start()
                return carry

            lax.fori_loop(nt[0], MAX_TILES, per_tile, 0)


def _ffn_kernel(te_ref, nt_ref, xs_ref, wg_ref, wu_ref, wd_ref, ys_ref, wg_scr, wu_scr, wd_scr):
    i = pl.program_id(0)
    first = (i == 0) | (te_ref[i] != te_ref[jnp.maximum(i - 1, 0)])

    @pl.when(first)
    def _():
        wg_scr[...] = wg_ref[0, 0].astype(BF16)
        wu_scr[...] = wu_ref[0, 0].astype(BF16)
        wd_scr[...] = wd_ref[0, 0].astype(BF16)

    @pl.when(i < nt_ref[0])
    def _():
        x = _load_rt(xs_ref, TE).astype(BF16)
        a = jnp.dot(x, wg_scr[...], preferred_element_type=F32)
        b = jnp.dot(x, wu_scr[...], preferred_element_type=F32)
        hdn = (a * jax.nn.sigmoid(a)) * b
        _store_rt(ys_ref, jnp.dot(hdn.astype(BF16), wd_scr[...], preferred_element_type=F32), TE)

    @pl.when(i >= nt_ref[0])
    def _():
        ys_ref[...] = jnp.zeros_like(ys_ref)


def _combine_kernel(final, ntab, lbtab, gdtab, lpos_ref, w_ref, h_ref, gf_ref, ys_ref, *rest):
    if final:
        yp_ref, ysamp_ref, yl_scr, o_scr, sems = rest
        dst = o_scr
    else:
        out_ref, yl_scr, sems = rest
        dst = out_ref
    b = pl.program_id(0)
    slot = b % 2

    def fetch(blk, sl):
        def per_expert(e, carry):
            _segment_copies(ys_ref, gdtab[blk, e], yl_scr.at[sl], lbtab[blk, e], ntab[blk, e], sems.at[sl], False)
            return carry

        lax.fori_loop(0, N_EXPERTS, per_expert, 0)

    @pl.when(b == 0)
    def _():
        fetch(0, 0)

    @pl.when(b + 1 < pl.num_programs(0))
    def _():
        fetch(b + 1, 1 - slot)

    pltpu.make_async_copy(ys_ref.at[_rows(0, 2 * TR)], yl_scr.at[slot], sems.at[slot]).wait()

    def merge(t, carry):
        ya = yl_scr[slot, _rows(lpos_ref[0, 0, t], 1), :]
        yb = yl_scr[slot, _rows(lpos_ref[0, 1, t], 1), :]
        dst[_rows(t, 1), :] = h_ref[_rows(t, 1), :] + (w_ref[0, 0, t] * ya + w_ref[0, 1, t] * yb)
        return carry

    lax.fori_loop(0, TR, merge, 0, unroll=8)
    if final:
        y = _rmsnorm(_load_rt(o_scr, TR), gf_ref[...])

        @pl.when(b < T_PROMPT // TR)
        def _():
            yp_ref[...] = y

        @pl.when(b >= T_PROMPT // TR)
        def _():
            ysamp_ref[...] = y


def _moe_layer(h_all, g, wr, br, layer, wg, wu, wd, gf, upper, ltri, final):
    n_r = T_ALL // TR
    h_blk = pl.BlockSpec((TR * SUB, LANES), lambda i, *_: (i, 0))
    eid, wts = pl.pallas_call(
        _router_kernel,
        grid=(n_r,),
        in_specs=[h_blk, _const((1, D_MODEL)), _const((D_MODEL, LANES)), _const((1, LANES))],
        out_specs=[pl.BlockSpec((2, TR), lambda i: (0, i)), pl.BlockSpec((2, TR), lambda i: (0, i))],
        out_shape=[jax.ShapeDtypeStruct((2, T_ALL), I32), jax.ShapeDtypeStruct((2, T_ALL), F32)],
        compiler_params=_params(),
        name="router",
    )(h_all, g, wr, br)

    tab = jax.ShapeDtypeStruct((n_r, 1, LANES), I32)
    tab_blk = pl.BlockSpec((1, 1, LANES), lambda p, i: (i * p, 0, 0))
    lpos, ntab, lbtab, gdtab, te, nt, padstart, padlen = pl.pallas_call(
        _plan_kernel,
        grid=(2, n_r),
        in_specs=[pl.BlockSpec((2, TR), lambda p, i: (0, i)), _const((TR, TR)), _const((N_EXPERTS, N_EXPERTS))],
        out_specs=[pl.BlockSpec((2, TR), lambda p, i: (0, i * p)), tab_blk, tab_blk, tab_blk,
                   _const((1, 256)), _const((1, LANES)), _const((1, LANES)), _const((1, LANES))],
        out_shape=[jax.ShapeDtypeStruct((2, T_ALL), I32), tab, tab, tab, jax.ShapeDtypeStruct((1, 256), I32),
                   jax.ShapeDtypeStruct((1, LANES), I32), jax.ShapeDtypeStruct((1, LANES), I32),
                   jax.ShapeDtypeStruct((1, LANES), I32)],
        scratch_shapes=[pltpu.VMEM((N_EXPERTS, LANES), F32), pltpu.VMEM((N_EXPERTS, LANES), F32)],
        compiler_params=_params(2),
        name="plan",
    )(eid, upper, ltri)

    seg_tabs = (ntab.reshape(n_r, LANES), lbtab.reshape(n_r, LANES), gdtab.reshape(n_r, LANES))
    lpos3 = lpos.reshape(2, n_r, TR).transpose(1, 0, 2)
    wts3 = wts.reshape(2, n_r, TR).transpose(1, 0, 2)
    smem_blk = pl.BlockSpec((1, 2, TR), lambda i, *_: (i, 0, 0), memory_space=pltpu.SMEM)

    xs = pl.pallas_call(
        _dispatch_kernel,
        grid_spec=pltpu.PrefetchScalarGridSpec(
            num_scalar_prefetch=6,
            grid=(n_r,),
            in_specs=[smem_blk, h_blk, _const((1, D_MODEL))],
            out_specs=pl.BlockSpec(memory_space=pl.ANY),
            scratch_shapes=[pltpu.VMEM((TR * SUB, LANES), F32), pltpu.VMEM((2, 2 * TR * SUB, LANES), F32),
                            pltpu.VMEM((TE * SUB, LANES), F32), pltpu.SemaphoreType.DMA((2,))]),
        out_shape=jax.ShapeDtypeStruct((P_ROWS * SUB, LANES), F32),
        compiler_params=_params(),
        name="dispatch",
    )(*seg_tabs, padstart.reshape(LANES), padlen.reshape(LANES), nt[0, 0:1], lpos3, h_all, g)

    def row_map(i, te_ref, nt_ref):
        return (jnp.minimum(i, nt_ref[0] - 1), 0)

    ys = pl.pallas_call(
        _ffn_kernel,
        grid_spec=pltpu.PrefetchScalarGridSpec(
            num_scalar_prefetch=2,
            grid=(MAX_TILES,),
            in_specs=[pl.BlockSpec((TE * SUB, LANES), row_map),
                      pl.BlockSpec((1, 1, D_MODEL, EXPERT_FF), lambda i, te_ref, nt_ref: (layer, te_ref[i], 0, 0)),
                      pl.BlockSpec((1, 1, D_MODEL, EXPERT_FF), lambda i, te_ref, nt_ref: (layer, te_ref[i], 0, 0)),
                      pl.BlockSpec((1, 1, EXPERT_FF, D_MODEL), lambda i, te_ref, nt_ref: (layer, te_ref[i], 0, 0))],
            out_specs=pl.BlockSpec((TE * SUB, LANES), lambda i, te_ref, nt_ref: (i, 0)),
            scratch_shapes=[pltpu.VMEM((D_MODEL, EXPERT_FF), BF16), pltpu.VMEM((D_MODEL, EXPERT_FF), BF16),
                            pltpu.VMEM((EXPERT_FF, D_MODEL), BF16)]),
        out_shape=jax.ShapeDtypeStruct((P_ROWS * SUB, LANES), F32),
        compiler_params=_params(),
        name="expert_ffn",
    )(te.reshape(256), nt[0, 0:1], xs, wg, wu, wd)

    n_p = T_PROMPT // TR
    if final:
        out_specs = [pl.BlockSpec((TR, D_MODEL), lambda i, *_: (jnp.minimum(i, n_p - 1), 0)),
                     pl.BlockSpec((TR, D_MODEL), lambda i, *_: (jnp.maximum(i - n_p, 0), 0))]
        out_shape = [jax.ShapeDtypeStruct((T_PROMPT, D_MODEL), F32), jax.ShapeDtypeStruct((T_SAMPLE, D_MODEL), F32)]
        scratch = [pltpu.VMEM((2, 2 * TR * SUB, LANES), F32), pltpu.VMEM((TR * SUB, LANES), F32),
                   pltpu.SemaphoreType.DMA((2,))]
    else:
        out_specs = h_blk
        out_shape = jax.ShapeDtypeStruct((T_ALL * SUB, LANES), F32)
        scratch = [pltpu.VMEM((2, 2 * TR * SUB, LANES), F32), pltpu.SemaphoreType.DMA((2,))]
    return pl.pallas_call(
        functools.partial(_combine_kernel, final),
        grid_spec=pltpu.PrefetchScalarGridSpec(
            num_scalar_prefetch=3,
            grid=(n_r,),
            in_specs=[smem_blk, smem_blk, h_blk, _const((1, D_MODEL)), pl.BlockSpec(memory_space=pl.ANY)],
            out_specs=out_specs,
            scratch_shapes=scratch),
        out_shape=out_shape,
        compiler_params=_params(),
        name="combine",
    )(*seg_tabs, lpos3, wts3, h_all, gf, ys)


def kernel(x_prompt, x_sample, state_ret, cache_conv, norm_mix_g, w_in, conv_w, conv_b, conv_ln_g, conv_ln_b,
           w_out, norm_ffn_g, router_group_w, router_group_b, router_expert_w, router_expert_b, expert_w_gate,
           expert_w_up, expert_w_down, norm_final_g):
    h = (x_prompt.reshape(T_PROMPT, D_MODEL), x_sample.reshape(T_SAMPLE, D_MODEL))

    cos_p, sin_p = _rope_tables(jnp.arange(SEQ))
    cos_s, sin_s = _rope_tables(PAST_LEN + jnp.arange(DEC_SEQ))
    reps = TM // DEC_SEQ
    tabs_p = (cos_p, sin_p) + _retention_tables(C_PROMPT, 0)
    tabs_s = (jnp.tile(cos_s, (reps, 1)), jnp.tile(sin_s, (reps, 1))) + _retention_tables(DEC_SEQ, PAST_LEN)

    upper = (jnp.arange(TR)[:, None] < jnp.arange(TR)[None, :]).astype(BF16)
    ltri = (jnp.arange(N_EXPERTS)[:, None] >= jnp.arange(N_EXPERTS)[None, :]).astype(F32)
    pad_hist = HIST - (CONV_WIDTH - 1)
    gf = norm_final_g.reshape(1, D_MODEL)

    ret_p, conv_p, ret_s, conv_s = [], [], [], []
    for l in range(DEPTH):
        cw = jnp.repeat(jnp.pad(conv_w[l], ((0, HIST - CONV_WIDTH), (0, 0))), SUB, axis=0)
        c0 = jnp.pad(cache_conv[l], ((0, 0), (pad_hist, 0), (0, 0)))
        if l == 0:
            (win_hi, win_lo), (wout_hi, wout_lo) = _split(w_in[l]), _split(w_out[l])
            lo_weights = (win_lo, wout_lo)
        else:
            win_hi, wout_hi, lo_weights = w_in[l].astype(BF16), w_out[l].astype(BF16), ()
        hm, sp, cp, ss, cs = _mixer_layer(
            h, norm_mix_g[l].reshape(1, D_MODEL), win_hi, cw, conv_b[l].reshape(1, CONV_CH),
            conv_ln_g[l].reshape(1, CONV_CH), conv_ln_b[l].reshape(1, CONV_CH), wout_hi,
            state_ret[l], c0, tabs_p, tabs_s, lo_weights)
        ret_p.append(sp)
        conv_p.append(cp[:, pad_hist:])
        ret_s.append(ss)
        conv_s.append(cs[:, pad_hist:])

        wr = jnp.zeros((D_MODEL, LANES), F32)
        wr = wr.at[:, 0:N_GROUPS].set(router_group_w[l])
        wr = wr.at[:, EPG:EPG + N_EXPERTS].set(router_expert_w[l].transpose(1, 0, 2).reshape(D_MODEL, N_EXPERTS))
        br = jnp.zeros((1, LANES), F32)
        br = br.at[0, 0:N_GROUPS].set(router_group_b[l])
        br = br.at[0, EPG:EPG + N_EXPERTS].set(router_expert_b[l].reshape(N_EXPERTS))
        h = _moe_layer(hm, norm_ffn_g[l].reshape(1, D_MODEL), wr, br, l, expert_w_gate, expert_w_up,
                       expert_w_down, gf, upper, ltri, final=(l == DEPTH - 1))
        if l < DEPTH - 1:
            h = (h,)

    y_prompt = h[0].reshape(BATCH, SEQ, D_MODEL)
    y_sample = h[1].reshape(DEC_BATCH, DEC_SEQ, D_MODEL)
    return (y_prompt, y_sample, jnp.stack(ret_p), jnp.stack(conv_p), jnp.stack(ret_s), jnp.stack(conv_s))
```

```python
import functools

import jax
import jax.numpy as jnp
from jax import lax
from jax.experimental import pallas as pl
from jax.experimental.pallas import tpu as pltpu

F32 = jnp.float32
BF16 = jnp.bfloat16
I32 = jnp.int32

D_MODEL = 1024
DEPTH = 2
BATCH, SEQ = 2, 8192
DEC_BATCH, DEC_SEQ = 16, 64
PAST_LEN = 1024
RET_HEADS, RET_DK, RET_DV = 4, 64, 128
QK_W = RET_HEADS * RET_DK
V_W = RET_HEADS * RET_DV
CONV_CH = 512
CONV_WIDTH = 31
HIST = 32
IN_COLS = 2 * QK_W + 2 * V_W + 2 * CONV_CH
N_GROUPS, EPG = 4, 8
N_EXPERTS = N_GROUPS * EPG
EXPERT_FF = 512
ROPE_BASE = 10000.0
EPS = 1e-6

T_PROMPT = BATCH * SEQ
T_SAMPLE = DEC_BATCH * DEC_SEQ
T_ALL = T_PROMPT + T_SAMPLE

SUB, LANES = 8, 128
TM = 512
C_PROMPT = 256
TR = 1024
TE = 512
SEG = 64
N_SLOTS = 2 * T_ALL
MAX_TILES = N_SLOTS // TE + N_EXPERTS
P_ROWS = MAX_TILES * TE
VMEM_LIMIT = 60000 * 1024
N_PROMPT_STEPS = T_PROMPT // TM

SPLIT_FACTOR = float(2 ** 16 + 1)
_NN = (((1,), (0,)), ((), ()))
_NT = (((1,), (1,)), ((), ()))
_TN = (((0,), (0,)), ((), ()))


def _params(n_axes=1):
    return pltpu.CompilerParams(dimension_semantics=("arbitrary",) * n_axes,
                                vmem_limit_bytes=VMEM_LIMIT)


def _const(shape):
    nd = len(shape)
    return pl.BlockSpec(shape, lambda *_: (0,) * nd)


def _resident(shape):
    nd = len(shape)
    return pl.BlockSpec(shape, lambda *_: (0,) * nd, pipeline_mode=pl.Buffered(1))


def _load_rt(ref, rows):
    return jnp.concatenate([ref[pl.ds(s, rows, stride=SUB), :] for s in range(SUB)], axis=1)


def _store_rt(ref, val, rows):
    for s in range(SUB):
        ref[pl.ds(s, rows, stride=SUB), :] = val[:, s * LANES:(s + 1) * LANES]


def _split(a):
    t = a * SPLIT_FACTOR
    hi = t - (t - a)
    return hi.astype(BF16), (a - hi).astype(BF16)


def _mm(a, b, dims, precise):
    if not precise:
        return lax.dot_general(a.astype(BF16), b.astype(BF16), dims, preferred_element_type=F32)
    ah, al = _split(a)
    bh, bl = _split(b)
    return (lax.dot_general(ah, bh, dims, preferred_element_type=F32)
            + (lax.dot_general(al, bh, dims, preferred_element_type=F32)
               + lax.dot_general(ah, bl, dims, preferred_element_type=F32)))


def _mm_w(a, whi, wlo):
    if wlo is None:
        return jnp.dot(a.astype(BF16), whi, preferred_element_type=F32)
    ah, al = _split(a)
    return (jnp.dot(ah, whi, preferred_element_type=F32)
            + (jnp.dot(al, whi, preferred_element_type=F32) + jnp.dot(ah, wlo, preferred_element_type=F32)))


def _mixer_body(c, ns, carry, steps_per_seq, x_is_rt, precise,
                x_ref, g_ref, win_ref, winlo_ref, cos_ref, sin_ref, intra_ref, qdec_ref, kdec_ref, decs_ref,
                masks_ref, cw_ref, cb_ref, lng_ref, lnb_ref, wout_ref, woutlo_ref, s0_ref, c0_ref,
                out_ref, snew_ref, cnew_ref, route_refs, s_scr, ubuf, p_scr, cv_scr, cat_scr):
    step = pl.program_id(0)
    x = _load_rt(x_ref, TM) if x_is_rt else x_ref[...]
    xn = x * lax.rsqrt(jnp.mean(x * x, axis=-1, keepdims=True) + EPS) * g_ref[...]
    xn_parts = _split(xn) if precise else (xn.astype(BF16), None)

    def proj(lo, hi):
        out = jnp.dot(xn_parts[0], win_ref[:, lo:hi], preferred_element_type=F32)
        if precise:
            out = out + (jnp.dot(xn_parts[1], win_ref[:, lo:hi], preferred_element_type=F32)
                         + jnp.dot(xn_parts[0], winlo_ref[:, lo:hi], preferred_element_type=F32))
        return out

    q = proj(0, QK_W)
    k = proj(QK_W, 2 * QK_W)
    v = proj(2 * QK_W, 2 * QK_W + V_W)
    gate = proj(2 * QK_W + V_W, 2 * QK_W + 2 * V_W)
    ca = proj(2 * QK_W + 2 * V_W, 2 * QK_W + 2 * V_W + CONV_CH)
    cb = proj(2 * QK_W + 2 * V_W + CONV_CH, IN_COLS)

    lane = lax.broadcasted_iota(I32, (TM, QK_W), 1)
    first_half = (lane & (RET_DK // 2)) == 0

    def rope(t):
        partner = jnp.where(first_half, pltpu.roll(t, QK_W - RET_DK // 2, 1), pltpu.roll(t, RET_DK // 2, 1))
        return t * cos_ref[...] + partner * sin_ref[...]

    q = rope(q)
    k = rope(k) * (RET_DK ** -0.5)

    if carry:
        @pl.when(step % steps_per_seq == 0)
        def _():
            s_scr[...] = jnp.zeros_like(s_scr)
            ubuf[0:HIST, :] = jnp.zeros((HIST, CONV_CH), F32)
    else:
        s_scr[...] = jnp.zeros_like(s_scr)

    lane_c = lax.broadcasted_iota(I32, (c, QK_W), 1)
    for s in range(ns):
        r0 = s * c
        qc, kc, vc = q[r0:r0 + c], k[r0:r0 + c], v[r0:r0 + c]
        if not carry:
            for h in range(RET_HEADS):
                s_scr[h * RET_DK:(h + 1) * RET_DK, h * RET_DV:(h + 1) * RET_DV] = s0_ref[s, h]
        state = s_scr[...]
        o_cross = _mm(qc, state, _NN, precise) * qdec_ref[...]
        heads = []
        for h in range(RET_HEADS):
            qh = jnp.where((lane_c >> 6) == h, qc, 0.0)
            scores = _mm(qh, kc, _NT, precise) * intra_ref[h]
            heads.append(_mm(scores, vc[:, h * RET_DV:(h + 1) * RET_DV], _NN, precise))
        o = jnp.concatenate(heads, axis=1) + o_cross
        kv = _mm(kc * kdec_ref[...], vc, _TN, precise)
        s_new = decs_ref[...] * state + masks_ref[...] * kv
        if carry:
            s_scr[...] = s_new
            if s == ns - 1:
                @pl.when(step % steps_per_seq == steps_per_seq - 1)
                def _():
                    for h in range(RET_HEADS):
                        snew_ref[0, h] = s_new[h * RET_DK:(h + 1) * RET_DK, h * RET_DV:(h + 1) * RET_DV]
        else:
            for h in range(RET_HEADS):
                snew_ref[s, h] = s_new[h * RET_DK:(h + 1) * RET_DK, h * RET_DV:(h + 1) * RET_DV]

        normed = []
        for h in range(RET_HEADS):
            oh = o[:, h * RET_DV:(h + 1) * RET_DV]
            normed.append(oh * lax.rsqrt(jnp.mean(oh * oh, axis=-1, keepdims=True) + EPS))
        gc = gate[r0:r0 + c]
        og = jnp.concatenate(normed, axis=1) * (gc * jax.nn.sigmoid(gc))

        u = ca[r0:r0 + c] * jax.nn.sigmoid(cb[r0:r0 + c])
        if not carry:
            ubuf[0:HIST, :] = c0_ref[s]
        ubuf[HIST:HIST + c, :] = u
        off = HIST - (CONV_WIDTH - 1)
        for b in range(SUB):
            nrow = c + (2 * SUB if off + b + c - 1 >= c + SUB else SUB)
            for rb in range(0, nrow, 32):
                nr = min(32, nrow - rb)
                acc = None
                for j in range(b, CONV_WIDTH, SUB):
                    wj = jnp.concatenate([cw_ref[j * SUB:(j + 1) * SUB, :]] * (nr // SUB), axis=0)
                    term = ubuf[rb + j - b:rb + j - b + nr, :] * wj
                    acc = term if acc is None else acc + term
                p_scr[b, rb:rb + nr, :] = acc
        for rb in range(0, c, 32):
            acc = jnp.broadcast_to(cb_ref[...], (32, CONV_CH))
            for b in range(SUB):
                acc = acc + p_scr[b, rb + off + b:rb + off + b + 32, :]
            cv_scr[rb:rb + 32, :] = acc
        tail = ubuf[c:c + HIST, :]
        if carry:
            ubuf[0:HIST, :] = tail
            if s == ns - 1:
                @pl.when(step % steps_per_seq == steps_per_seq - 1)
                def _():
                    cnew_ref[0] = tail
        else:
            cnew_ref[s] = tail
        cv = cv_scr[0:c, :]
        mu = jnp.mean(cv, axis=-1, keepdims=True)
        xc = cv - mu
        var = jnp.mean(xc * xc, axis=-1, keepdims=True)
        cn = xc * lax.rsqrt(var + EPS) * lng_ref[...] + lnb_ref[...]
        cact = cn * jax.nn.sigmoid(cn)
        cat_scr[r0:r0 + c, 0:V_W] = og
        cat_scr[r0:r0 + c, V_W:D_MODEL] = cact

    h_new = x + _mm_w(cat_scr[...], wout_ref[...], woutlo_ref[...] if precise else None)
    _store_rt(out_ref, h_new, TM)
    _route(h_new, *route_refs)


def _mixer_kernel(x_is_rt, precise, *refs):
    if x_is_rt:
        xp_ref = xs_ref = refs[0]
        refs = refs[1:]
    else:
        xp_ref, xs_ref = refs[0], refs[1]
        refs = refs[2:]
    if precise:
        winlo_ref, woutlo_ref = refs[0], refs[1]
        refs = refs[2:]
    else:
        winlo_ref = woutlo_ref = None
    (g_ref, win_ref, cw_ref, cb_ref, lng_ref, lnb_ref, wout_ref, masks_ref,
     cos_p, sin_p, intra_p, qdec_p, kdec_p, decs_p,
     cos_s, sin_s, intra_s, qdec_s, kdec_s, decs_s, s0_ref, c0_ref, gffn_ref, wr_ref, br_ref,
     out_ref, snew_p, cnew_p, snew_s, cnew_s, eid_ref, wts_ref, s_scr, ubuf, p_scr, cv_scr, cat_scr) = refs
    route_refs = (gffn_ref, wr_ref, br_ref, eid_ref, wts_ref)
    step = pl.program_id(0)

    @pl.when(step < N_PROMPT_STEPS)
    def _():
        _mixer_body(C_PROMPT, TM // C_PROMPT, True, SEQ // TM, x_is_rt, precise,
                    xp_ref, g_ref, win_ref, winlo_ref, cos_p, sin_p, intra_p, qdec_p, kdec_p, decs_p,
                    masks_ref, cw_ref, cb_ref, lng_ref, lnb_ref, wout_ref, woutlo_ref, None, None,
                    out_ref, snew_p, cnew_p, route_refs, s_scr, ubuf, p_scr, cv_scr, cat_scr)

    @pl.when(step >= N_PROMPT_STEPS)
    def _():
        _mixer_body(DEC_SEQ, TM // DEC_SEQ, False, 1, x_is_rt, precise,
                    xs_ref, g_ref, win_ref, winlo_ref, cos_s, sin_s, intra_s, qdec_s, kdec_s, decs_s,
                    masks_ref, cw_ref, cb_ref, lng_ref, lnb_ref, wout_ref, woutlo_ref, s0_ref, c0_ref,
                    out_ref, snew_s, cnew_s, route_refs, s_scr, ubuf, p_scr, cv_scr, cat_scr)


def _retention_tables(c, pos0):
    h = jnp.arange(RET_HEADS, dtype=F32)
    log_g = jnp.log1p(-jnp.exp2(-5.0 - h))
    idx = jnp.arange(c, dtype=F32)
    rel = idx[:, None] - idx[None, :]
    intra = jnp.where(rel >= 0, jnp.exp(log_g[:, None, None] * jnp.maximum(rel, 0.0)), 0.0)
    q_dec = jnp.exp(log_g[None, :] * (idx + 1.0)[:, None])
    k_dec = jnp.exp(log_g[None, :] * (c - 1.0 - idx)[:, None])
    chunk_dec = jnp.exp(log_g * c)
    qdec = jnp.repeat(q_dec, RET_DV, axis=1)
    kdec = jnp.repeat(k_dec, RET_DK, axis=1)
    row_h = jnp.arange(QK_W) // RET_DK
    col_h = jnp.arange(V_W) // RET_DV
    masks = (row_h[:, None] == col_h[None, :]).astype(F32)
    decs = masks * chunk_dec[row_h][:, None]
    return intra, qdec, kdec, decs, masks


def _rope_tables(pos):
    half = RET_DK // 2
    inv = ROPE_BASE ** (-jnp.arange(half, dtype=F32) / half)
    ang = pos.astype(F32)[:, None] * inv[None, :]
    cos, sin = jnp.cos(ang), jnp.sin(ang)
    cos_t = jnp.tile(jnp.concatenate([cos, cos], axis=1), (1, RET_HEADS))
    sin_t = jnp.tile(jnp.concatenate([-sin, sin], axis=1), (1, RET_HEADS))
    return cos_t, sin_t


def _mixer_layer(xs, g, win, cw, cb, lng, lnb, wout, s0, c0, tabs_p, tabs_s, gffn, wr, br, lo_weights=()):
    precise = len(lo_weights) == 2
    lo_specs = [_resident((D_MODEL, IN_COLS)), _resident((D_MODEL, D_MODEL))] if precise else []
    x_is_rt = len(xs) == 1
    if x_is_rt:
        x_specs = [pl.BlockSpec((TM * SUB, LANES), lambda i: (i, 0))]
    else:
        x_specs = [pl.BlockSpec((TM, D_MODEL), lambda i: (jnp.minimum(i, N_PROMPT_STEPS - 1), 0)),
                   pl.BlockSpec((TM, D_MODEL), lambda i: (jnp.maximum(i - N_PROMPT_STEPS, 0), 0))]
    cos_p, sin_p, intra_p, qdec_p, kdec_p, decs_p, masks = tabs_p
    cos_s, sin_s, intra_s, qdec_s, kdec_s, decs_s, _ = tabs_s
    steps = SEQ // TM
    ns = TM // DEC_SEQ

    def table_specs(c):
        return [_resident((RET_HEADS, c, c)), _resident((c, V_W)), _resident((c, QK_W)), _resident((QK_W, V_W))]

    def sample_blk(i):
        return jnp.maximum(i - N_PROMPT_STEPS, 0)

    def prompt_blk(i):
        return jnp.minimum(i // steps, BATCH - 1)

    def prompt_pos(i):
        return (jnp.minimum(i, N_PROMPT_STEPS - 1) % steps, 0)

    return pl.pallas_call(
        functools.partial(_mixer_kernel, x_is_rt, precise),
        grid=(T_ALL // TM,),
        in_specs=x_specs + lo_specs + [_const((1, D_MODEL)), _resident((D_MODEL, IN_COLS)),
                  _resident((HIST * SUB, CONV_CH)), _const((1, CONV_CH)), _const((1, CONV_CH)), _const((1, CONV_CH)),
                  _resident((D_MODEL, D_MODEL)), _resident((QK_W, V_W)),
                  pl.BlockSpec((TM, QK_W), prompt_pos), pl.BlockSpec((TM, QK_W), prompt_pos)]
        + table_specs(C_PROMPT) + [_resident((TM, QK_W)), _resident((TM, QK_W))] + table_specs(DEC_SEQ)
        + [pl.BlockSpec((ns, RET_HEADS, RET_DK, RET_DV), lambda i: (sample_blk(i), 0, 0, 0)),
           pl.BlockSpec((ns, HIST, CONV_CH), lambda i: (sample_blk(i), 0, 0)),
           _const((1, D_MODEL)), _resident((D_MODEL, LANES)), _const((1, LANES))],
        out_specs=[pl.BlockSpec((TM * SUB, LANES), lambda i: (i, 0)),
                   pl.BlockSpec((1, RET_HEADS, RET_DK, RET_DV), lambda i: (prompt_blk(i), 0, 0, 0)),
                   pl.BlockSpec((1, HIST, CONV_CH), lambda i: (prompt_blk(i), 0, 0)),
                   pl.BlockSpec((ns, RET_HEADS, RET_DK, RET_DV), lambda i: (sample_blk(i), 0, 0, 0)),
                   pl.BlockSpec((ns, HIST, CONV_CH), lambda i: (sample_blk(i), 0, 0)),
                   pl.BlockSpec((2, TM), lambda i: (0, i)), pl.BlockSpec((2, TM), lambda i: (0, i))],
        out_shape=[jax.ShapeDtypeStruct((T_ALL * SUB, LANES), F32),
                   jax.ShapeDtypeStruct((BATCH, RET_HEADS, RET_DK, RET_DV), F32),
                   jax.ShapeDtypeStruct((BATCH, HIST, CONV_CH), F32),
                   jax.ShapeDtypeStruct((DEC_BATCH, RET_HEADS, RET_DK, RET_DV), F32),
                   jax.ShapeDtypeStruct((DEC_BATCH, HIST, CONV_CH), F32),
                   jax.ShapeDtypeStruct((2, T_ALL), I32), jax.ShapeDtypeStruct((2, T_ALL), F32)],
        scratch_shapes=[pltpu.VMEM((QK_W, V_W), F32), pltpu.VMEM((HIST + C_PROMPT, CONV_CH), F32),
                        pltpu.VMEM((SUB, C_PROMPT + 2 * SUB, CONV_CH), F32),
                        pltpu.VMEM((C_PROMPT, CONV_CH), F32), pltpu.VMEM((TM, D_MODEL), F32)],
        compiler_params=_params(),
        name="mixer",
    )(*xs, *lo_weights, g, win, cw, cb, lng, lnb, wout, masks, cos_p, sin_p, intra_p, qdec_p, kdec_p, decs_p,
      cos_s, sin_s, intra_s, qdec_s, kdec_s, decs_s, s0, c0, gffn, wr, br)


def _rmsnorm(x, g):
    return x * lax.rsqrt(jnp.mean(x * x, axis=-1, keepdims=True) + EPS) * g


def _route(h, g_ref, wr_ref, br_ref, eid_ref, wts_ref):
    n = h.shape[0]
    xn = _rmsnorm(h, g_ref[...])
    logits = (jnp.dot(xn, wr_ref[...], preferred_element_type=F32, precision=lax.Precision.HIGHEST)
              + br_ref[...]).T
    best = logits[0:1, :]
    gidx = jnp.zeros((1, n), I32)
    for r in range(1, N_GROUPS):
        row = logits[r:r + 1, :]
        upd = row > best
        gidx = jnp.where(upd, r, gidx)
        best = jnp.where(upd, row, best)
    denom = jnp.zeros((1, n), F32)
    for r in range(N_GROUPS):
        denom = denom + jnp.exp(logits[r:r + 1, :] - best)
    pgate = 1.0 / denom
    le = logits[EPG:2 * EPG, :]
    for gi in range(1, N_GROUPS):
        le = jnp.where(gidx == gi, logits[EPG * (gi + 1):EPG * (gi + 2), :], le)
    sub = lax.broadcasted_iota(I32, (EPG, n), 0)
    v1 = jnp.max(le, axis=0, keepdims=True)
    i1 = jnp.min(jnp.where(le == v1, sub, EPG), axis=0, keepdims=True)
    le2 = jnp.where(sub == i1, -jnp.inf, le)
    v2 = jnp.max(le2, axis=0, keepdims=True)
    i2 = jnp.min(jnp.where(le2 == v2, sub, EPG), axis=0, keepdims=True)
    e2 = jnp.exp(v2 - v1)
    den2 = 1.0 + e2
    wa = (1.0 / den2) * pgate
    wb = (e2 / den2) * pgate
    eid_ref[0:1, :] = gidx * EPG + i1
    eid_ref[1:2, :] = gidx * EPG + i2
    wts_ref[0:1, :] = wa
    wts_ref[1:2, :] = wb


def _plan_kernel(eid_ref, upper_ref, ltri_ref, lpos_ref, ntab_ref, lbtab_ref, gdtab_ref, te_ref, nt_ref,
                 padstart_ref, padlen_ref, cnt_scr, run_scr):
    phase = pl.program_id(0)
    i = pl.program_id(1)
    rows = lax.broadcasted_iota(I32, (N_EXPERTS, TR), 0)
    oh0 = (rows == eid_ref[0:1, :]).astype(F32)
    oh1 = (rows == eid_ref[1:2, :]).astype(F32)
    tot0 = jnp.sum(oh0, axis=1, keepdims=True)
    tot1 = jnp.sum(oh1, axis=1, keepdims=True)
    eye = (lax.broadcasted_iota(I32, (N_EXPERTS, LANES), 0)
           == lax.broadcasted_iota(I32, (N_EXPERTS, LANES), 1)).astype(F32)

    def to_row(col):
        return jnp.sum(col * eye, axis=0, keepdims=True).astype(I32)

    @pl.when((phase == 0) & (i == 0))
    def _():
        cnt_scr[...] = jnp.zeros_like(cnt_scr)

    @pl.when(phase == 0)
    def _():
        cnt_scr[...] = cnt_scr[...] + (tot0 + tot1)

    @pl.when((phase == 1) & (i == 0))
    def _():
        cnt = cnt_scr[...].astype(I32)
        ntile = ((cnt + (TE - 1)) >> (TE.bit_length() - 1)).astype(F32)
        incl = jnp.dot(ltri_ref[...], ntile, preferred_element_type=F32,
                       precision=lax.Precision.HIGHEST)
        run_scr[...] = (incl - ntile) * float(TE)
        ntot = incl[N_EXPERTS - 1:N_EXPERTS, 0:1]
        j = lax.broadcasted_iota(I32, (N_EXPERTS, 256), 1).astype(F32)
        j = jnp.minimum(j, ntot - 1.0)
        te = jnp.sum((incl[:, 0:1] <= j).astype(F32), axis=0, keepdims=True)
        te_ref[...] = te.astype(I32)
        nt_ref[...] = incl[N_EXPERTS - 1:N_EXPERTS, :].astype(I32)
        cntf = cnt_scr[:, 0:1]
        padstart_ref[...] = to_row((incl[:, 0:1] - ntile[:, 0:1]) * float(TE) + cntf)
        padlen_ref[...] = to_row(ntile[:, 0:1] * float(TE) - cntf)

    @pl.when(phase == 1)
    def _():
        run = run_scr[:, 0:1]
        tot = tot0 + tot1
        lbase = jnp.dot(ltri_ref[...], jnp.broadcast_to(tot, (N_EXPERTS, LANES)), preferred_element_type=F32,
                        precision=lax.Precision.HIGHEST)[:, 0:1] - tot
        ex0 = jnp.dot(oh0.astype(BF16), upper_ref[...], preferred_element_type=F32)
        ex1 = jnp.dot(oh1.astype(BF16), upper_ref[...], preferred_element_type=F32)
        p0 = jnp.sum(oh0 * (lbase + ex0), axis=0, keepdims=True)
        p1 = jnp.sum(oh1 * (lbase + tot0 + ex1), axis=0, keepdims=True)
        lpos_ref[0:1, :] = p0.astype(I32)
        lpos_ref[1:2, :] = p1.astype(I32)
        ntab_ref[0] = to_row(tot)
        lbtab_ref[0] = to_row(lbase)
        gdtab_ref[0] = to_row(run)
        run_scr[...] = run_scr[...] + tot


def _rows(start, n):
    return pl.ds(pl.multiple_of(start * SUB, SUB), n * SUB)


def _segment_copies(src_ref, src0, dst_ref, dst0, n, sem, wait):
    def piece(off, size):
        cp = pltpu.make_async_copy(src_ref.at[_rows(src0 + off, size)], dst_ref.at[_rows(dst0 + off, size)], sem)
        if wait:
            cp.wait()
        else:
            cp.start()

    nfull = n >> (SEG.bit_length() - 1)

    def full(j, carry):
        piece(j * SEG, SEG)
        return carry

    lax.fori_loop(0, nfull, full, 0)
    rem = n - nfull * SEG
    bit = SEG // 2
    while bit >= 1:
        @pl.when((rem & bit) != 0)
        def _(bit=bit):
            piece(nfull * SEG + (rem & ~(2 * bit - 1)), bit)
        bit //= 2


def _dispatch_kernel(ntab, lbtab, gdtab, padstart, padlen, nt, lpos_ref, h_ref, g_ref, xs_ref,
                     xn_scr, ls_scr, zero_scr, sems):
    b = pl.program_id(0)
    last = pl.num_programs(0) - 1
    slot = b % 2
    _store_rt(xn_scr, _rmsnorm(_load_rt(h_ref, TR), g_ref[...]), TR)

    def place(t, carry):
        row = xn_scr[_rows(t, 1), :]
        ls_scr[slot, _rows(lpos_ref[0, 0, t], 1), :] = row
        ls_scr[slot, _rows(lpos_ref[0, 1, t], 1), :] = row
        return carry

    lax.fori_loop(0, TR, place, 0, unroll=8)

    def per_expert(e, carry):
        _segment_copies(ls_scr.at[slot], lbtab[b, e], xs_ref, gdtab[b, e], ntab[b, e], sems.at[slot], False)
        return carry

    lax.fori_loop(0, N_EXPERTS, per_expert, 0)

    def wait_block(sl):
        pltpu.make_async_copy(ls_scr.at[sl], xs_ref.at[_rows(0, 2 * TR)], sems.at[sl]).wait()

    @pl.when(b > 0)
    def _():
        wait_block(1 - slot)

    @pl.when(b == last)
    def _():
        wait_block(slot)
        zero_scr[...] = jnp.zeros_like(zero_scr)
        for wait in (False, True):
            def per_expert(e, carry, wait=wait):
                _segment_copies(zero_scr, 0, xs_ref, padstart[e], padlen[e], sems.at[0], wait)
                return carry

            lax.fori_loop(0, N_EXPERTS, per_expert, 0)

            def per_tile(j, carry, wait=wait):
                cp = pltpu.make_async_copy(zero_scr, xs_ref.at[_rows(j * TE, TE)], sems.at[0])
                if wait:
                    cp.wait()
                else:
                    cp.start()
                return carry

            lax.fori_loop(nt[0], MAX_TILES, per_tile, 0)


def _ffn_kernel(te_ref, nt_ref, xs_ref, wg_ref, wu_ref, wd_ref, ys_ref, wg_scr, wu_scr, wd_scr):
    i = pl.program_id(0)
    first = (i == 0) | (te_ref[i] != te_ref[jnp.maximum(i - 1, 0)])

    @pl.when(first)
    def _():
        wg_scr[...] = wg_ref[0, 0].astype(BF16)
        wu_scr[...] = wu_ref[0, 0].astype(BF16)
        wd_scr[...] = wd_ref[0, 0].astype(BF16)

    @pl.when(i < nt_ref[0])
    def _():
        x = _load_rt(xs_ref, TE).astype(BF16)
        a = jnp.dot(x, wg_scr[...], preferred_element_type=F32)
        b = jnp.dot(x, wu_scr[...], preferred_element_type=F32)
        hdn = (a * jax.nn.sigmoid(a)) * b
        _store_rt(ys_ref, jnp.dot(hdn.astype(BF16), wd_scr[...], preferred_element_type=F32), TE)

    @pl.when(i >= nt_ref[0])
    def _():
        ys_ref[...] = jnp.zeros_like(ys_ref)


def _combine_kernel(final, ntab, lbtab, gdtab, lpos_ref, w_ref, h_ref, gf_ref, ys_ref, *rest):
    if final:
        yp_ref, ysamp_ref, yl_scr, o_scr, sems = rest
        dst = o_scr
    else:
        out_ref, yl_scr, sems = rest
        dst = out_ref
    b = pl.program_id(0)
    slot = b % 2

    def fetch(blk, sl):
        def per_expert(e, carry):
            _segment_copies(ys_ref, gdtab[blk, e], yl_scr.at[sl], lbtab[blk, e], ntab[blk, e], sems.at[sl], False)
            return carry

        lax.fori_loop(0, N_EXPERTS, per_expert, 0)

    @pl.when(b == 0)
    def _():
        fetch(0, 0)

    @pl.when(b + 1 < pl.num_programs(0))
    def _():
        fetch(b + 1, 1 - slot)

    pltpu.make_async_copy(ys_ref.at[_rows(0, 2 * TR)], yl_scr.at[slot], sems.at[slot]).wait()

    def merge(t, carry):
        ya = yl_scr[slot, _rows(lpos_ref[0, 0, t], 1), :]
        yb = yl_scr[slot, _rows(lpos_ref[0, 1, t], 1), :]
        dst[_rows(t, 1), :] = h_ref[_rows(t, 1), :] + (w_ref[0, 0, t] * ya + w_ref[0, 1, t] * yb)
        return carry

    lax.fori_loop(0, TR, merge, 0, unroll=8)
    if final:
        y = _rmsnorm(_load_rt(o_scr, TR), gf_ref[...])

        @pl.when(b < T_PROMPT // TR)
        def _():
            yp_ref[...] = y

        @pl.when(b >= T_PROMPT // TR)
        def _():
            ysamp_ref[...] = y


def _moe_layer(h_all, eid, wts, g, layer, wg, wu, wd, gf, upper, ltri, final):
    n_r = T_ALL // TR
    h_blk = pl.BlockSpec((TR * SUB, LANES), lambda i, *_: (i, 0))

    tab = jax.ShapeDtypeStruct((n_r, 1, LANES), I32)
    tab_blk = pl.BlockSpec((1, 1, LANES), lambda p, i: (i * p, 0, 0))
    lpos, ntab, lbtab, gdtab, te, nt, padstart, padlen = pl.pallas_call(
        _plan_kernel,
        grid=(2, n_r),
        in_specs=[pl.BlockSpec((2, TR), lambda p, i: (0, i)), _const((TR, TR)), _const((N_EXPERTS, N_EXPERTS))],
        out_specs=[pl.BlockSpec((2, TR), lambda p, i: (0, i * p)), tab_blk, tab_blk, tab_blk,
                   _const((1, 256)), _const((1, LANES)), _const((1, LANES)), _const((1, LANES))],
        out_shape=[jax.ShapeDtypeStruct((2, T_ALL), I32), tab, tab, tab, jax.ShapeDtypeStruct((1, 256), I32),
                   jax.ShapeDtypeStruct((1, LANES), I32), jax.ShapeDtypeStruct((1, LANES), I32),
                   jax.ShapeDtypeStruct((1, LANES), I32)],
        scratch_shapes=[pltpu.VMEM((N_EXPERTS, LANES), F32), pltpu.VMEM((N_EXPERTS, LANES), F32)],
        compiler_params=_params(2),
        name="plan",
    )(eid, upper, ltri)

    seg_tabs = (ntab.reshape(n_r, LANES), lbtab.reshape(n_r, LANES), gdtab.reshape(n_r, LANES))
    lpos3 = lpos.reshape(2, n_r, TR).transpose(1, 0, 2)
    wts3 = wts.reshape(2, n_r, TR).transpose(1, 0, 2)
    smem_blk = pl.BlockSpec((1, 2, TR), lambda i, *_: (i, 0, 0), memory_space=pltpu.SMEM)

    xs = pl.pallas_call(
        _dispatch_kernel,
        grid_spec=pltpu.PrefetchScalarGridSpec(
            num_scalar_prefetch=6,
            grid=(n_r,),
            in_specs=[smem_blk, h_blk, _const((1, D_MODEL))],
            out_specs=pl.BlockSpec(memory_space=pl.ANY),
            scratch_shapes=[pltpu.VMEM((TR * SUB, LANES), F32), pltpu.VMEM((2, 2 * TR * SUB, LANES), F32),
                            pltpu.VMEM((TE * SUB, LANES), F32), pltpu.SemaphoreType.DMA((2,))]),
        out_shape=jax.ShapeDtypeStruct((P_ROWS * SUB, LANES), F32),
        compiler_params=_params(),
        name="dispatch",
    )(*seg_tabs, padstart.reshape(LANES), padlen.reshape(LANES), nt[0, 0:1], lpos3, h_all, g)

    def row_map(i, te_ref, nt_ref):
        return (jnp.minimum(i, nt_ref[0] - 1), 0)

    ys = pl.pallas_call(
        _ffn_kernel,
        grid_spec=pltpu.PrefetchScalarGridSpec(
            num_scalar_prefetch=2,
            grid=(MAX_TILES,),
            in_specs=[pl.BlockSpec((TE * SUB, LANES), row_map),
                      pl.BlockSpec((1, 1, D_MODEL, EXPERT_FF), lambda i, te_ref, nt_ref: (layer, te_ref[i], 0, 0)),
                      pl.BlockSpec((1, 1, D_MODEL, EXPERT_FF), lambda i, te_ref, nt_ref: (layer, te_ref[i], 0, 0)),
                      pl.BlockSpec((1, 1, EXPERT_FF, D_MODEL), lambda i, te_ref, nt_ref: (layer, te_ref[i], 0, 0))],
            out_specs=pl.BlockSpec((TE * SUB, LANES), lambda i, te_ref, nt_ref: (i, 0)),
            scratch_shapes=[pltpu.VMEM((D_MODEL, EXPERT_FF), BF16), pltpu.VMEM((D_MODEL, EXPERT_FF), BF16),
                            pltpu.VMEM((EXPERT_FF, D_MODEL), BF16)]),
        out_shape=jax.ShapeDtypeStruct((P_ROWS * SUB, LANES), F32),
        compiler_params=_params(),
        name="expert_ffn",
    )(te.reshape(256), nt[0, 0:1], xs, wg, wu, wd)

    n_p = T_PROMPT // TR
    if final:
        out_specs = [pl.BlockSpec((TR, D_MODEL), lambda i, *_: (jnp.minimum(i, n_p - 1), 0)),
                     pl.BlockSpec((TR, D_MODEL), lambda i, *_: (jnp.maximum(i - n_p, 0), 0))]
        out_shape = [jax.ShapeDtypeStruct((T_PROMPT, D_MODEL), F32), jax.ShapeDtypeStruct((T_SAMPLE, D_MODEL), F32)]
        scratch = [pltpu.VMEM((2, 2 * TR * SUB, LANES), F32), pltpu.VMEM((TR * SUB, LANES), F32),
                   pltpu.SemaphoreType.DMA((2,))]
    else:
        out_specs = h_blk
        out_shape = jax.ShapeDtypeStruct((T_ALL * SUB, LANES), F32)
        scratch = [pltpu.VMEM((2, 2 * TR * SUB, LANES), F32), pltpu.SemaphoreType.DMA((2,))]
    return pl.pallas_call(
        functools.partial(_combine_kernel, final),
        grid_spec=pltpu.PrefetchScalarGridSpec(
            num_scalar_prefetch=3,
            grid=(n_r,),
            in_specs=[smem_blk, smem_blk, h_blk, _const((1, D_MODEL)), pl.BlockSpec(memory_space=pl.ANY)],
            out_specs=out_specs,
            scratch_shapes=scratch),
        out_shape=out_shape,
        compiler_params=_params(),
        name="combine",
    )(*seg_tabs, lpos3, wts3, h_all, gf, ys)


def kernel(x_prompt, x_sample, state_ret, cache_conv, norm_mix_g, w_in, conv_w, conv_b, conv_ln_g, conv_ln_b,
           w_out, norm_ffn_g, router_group_w, router_group_b, router_expert_w, router_expert_b, expert_w_gate,
           expert_w_up, expert_w_down, norm_final_g):
    h = (x_prompt.reshape(T_PROMPT, D_MODEL), x_sample.reshape(T_SAMPLE, D_MODEL))

    cos_p, sin_p = _rope_tables(jnp.arange(SEQ))
    cos_s, sin_s = _rope_tables(PAST_LEN + jnp.arange(DEC_SEQ))
    reps = TM // DEC_SEQ
    tabs_p = (cos_p, sin_p) + _retention_tables(C_PROMPT, 0)
    tabs_s = (jnp.tile(cos_s, (reps, 1)), jnp.tile(sin_s, (reps, 1))) + _retention_tables(DEC_SEQ, PAST_LEN)

    upper = (jnp.arange(TR)[:, None] < jnp.arange(TR)[None, :]).astype(BF16)
    ltri = (jnp.arange(N_EXPERTS)[:, None] >= jnp.arange(N_EXPERTS)[None, :]).astype(F32)
    pad_hist = HIST - (CONV_WIDTH - 1)
    gf = norm_final_g.reshape(1, D_MODEL)

    ret_p, conv_p, ret_s, conv_s = [], [], [], []
    for l in range(DEPTH):
        cw = jnp.repeat(jnp.pad(conv_w[l], ((0, HIST - CONV_WIDTH), (0, 0))), SUB, axis=0)
        c0 = jnp.pad(cache_conv[l], ((0, 0), (pad_hist, 0), (0, 0)))
        if l == 0:
            (win_hi, win_lo), (wout_hi, wout_lo) = _split(w_in[l]), _split(w_out[l])
            lo_weights = (win_lo, wout_lo)
        else:
            win_hi, wout_hi, lo_weights = w_in[l].astype(BF16), w_out[l].astype(BF16), ()
        wr = jnp.zeros((D_MODEL, LANES), F32)
        wr = wr.at[:, 0:N_GROUPS].set(router_group_w[l])
        wr = wr.at[:, EPG:EPG + N_EXPERTS].set(router_expert_w[l].transpose(1, 0, 2).reshape(D_MODEL, N_EXPERTS))
        br = jnp.zeros((1, LANES), F32)
        br = br.at[0, 0:N_GROUPS].set(router_group_b[l])
        br = br.at[0, EPG:EPG + N_EXPERTS].set(router_expert_b[l].reshape(N_EXPERTS))
        gffn = norm_ffn_g[l].reshape(1, D_MODEL)
        hm, sp, cp, ss, cs, eid, wts = _mixer_layer(
            h, norm_mix_g[l].reshape(1, D_MODEL), win_hi, cw, conv_b[l].reshape(1, CONV_CH),
            conv_ln_g[l].reshape(1, CONV_CH), conv_ln_b[l].reshape(1, CONV_CH), wout_hi,
            state_ret[l], c0, tabs_p, tabs_s, gffn, wr, br, lo_weights)
        ret_p.append(sp)
        conv_p.append(cp[:, pad_hist:])
        ret_s.append(ss)
        conv_s.append(cs[:, pad_hist:])
        h = _moe_layer(hm, eid, wts, gffn, l, expert_w_gate, expert_w_up,
                       expert_w_down, gf, upper, ltri, final=(l == DEPTH - 1))
        if l < DEPTH - 1:
            h = (h,)

    y_prompt = h[0].reshape(BATCH, SEQ, D_MODEL)
    y_sample = h[1].reshape(DEC_BATCH, DEC_SEQ, D_MODEL)
    return (y_prompt, y_sample, jnp.stack(ret_p), jnp.stack(conv_p), jnp.stack(ret_s), jnp.stack(conv_s))
```

```python
import functools

import jax
import jax.numpy as jnp
from jax import lax
from jax.experimental import pallas as pl
from jax.experimental.pallas import tpu as pltpu

F32 = jnp.float32
BF16 = jnp.bfloat16
I32 = jnp.int32

D_MODEL = 1024
DEPTH = 2
BATCH, SEQ = 2, 8192
DEC_BATCH, DEC_SEQ = 16, 64
PAST_LEN = 1024
RET_HEADS, RET_DK, RET_DV = 4, 64, 128
QK_W = RET_HEADS * RET_DK
V_W = RET_HEADS * RET_DV
CONV_CH = 512
CONV_WIDTH = 31
HIST = 32
IN_COLS = 2 * QK_W + 2 * V_W + 2 * CONV_CH
N_GROUPS, EPG = 4, 8
N_EXPERTS = N_GROUPS * EPG
EXPERT_FF = 512
ROPE_BASE = 10000.0
EPS = 1e-6

T_PROMPT = BATCH * SEQ
T_SAMPLE = DEC_BATCH * DEC_SEQ
T_ALL = T_PROMPT + T_SAMPLE

SUB, LANES = 8, 128
TM = 512
C_PROMPT = 256
TR = 1024
TE = 512
SEG = 64
N_SLOTS = 2 * T_ALL
MAX_TILES = N_SLOTS // TE + N_EXPERTS
P_ROWS = MAX_TILES * TE
VMEM_LIMIT = 60000 * 1024
N_PROMPT_STEPS = T_PROMPT // TM

SPLIT_FACTOR = float(2 ** 16 + 1)
_NN = (((1,), (0,)), ((), ()))
_NT = (((1,), (1,)), ((), ()))
_TN = (((0,), (0,)), ((), ()))


def _params(n_axes=1):
    return pltpu.CompilerParams(dimension_semantics=("arbitrary",) * n_axes,
                                vmem_limit_bytes=VMEM_LIMIT)


def _const(shape):
    nd = len(shape)
    return pl.BlockSpec(shape, lambda *_: (0,) * nd)


def _resident(shape):
    nd = len(shape)
    return pl.BlockSpec(shape, lambda *_: (0,) * nd, pipeline_mode=pl.Buffered(1))


def _load_rt(ref, rows):
    return jnp.concatenate([ref[pl.ds(s, rows, stride=SUB), :] for s in range(SUB)], axis=1)


def _store_rt(ref, val, rows):
    for s in range(SUB):
        ref[pl.ds(s, rows, stride=SUB), :] = val[:, s * LANES:(s + 1) * LANES]


def _split(a):
    t = a * SPLIT_FACTOR
    hi = t - (t - a)
    return hi.astype(BF16), (a - hi).astype(BF16)


def _mm(a, b, dims, precise):
    if not precise:
        return lax.dot_general(a.astype(BF16), b.astype(BF16), dims, preferred_element_type=F32)
    ah, al = _split(a)
    bh, bl = _split(b)
    return (lax.dot_general(ah, bh, dims, preferred_element_type=F32)
            + (lax.dot_general(al, bh, dims, preferred_element_type=F32)
               + lax.dot_general(ah, bl, dims, preferred_element_type=F32)))


def _mm_w(a, whi, wlo):
    if wlo is None:
        return jnp.dot(a.astype(BF16), whi, preferred_element_type=F32)
    ah, al = _split(a)
    return (jnp.dot(ah, whi, preferred_element_type=F32)
            + (jnp.dot(al, whi, preferred_element_type=F32) + jnp.dot(ah, wlo, preferred_element_type=F32)))


def _mixer_body(c, ns, carry, steps_per_seq, x_is_rt, precise,
                x_ref, g_ref, win_ref, winlo_ref, cos_ref, sin_ref, intra_ref, qdec_ref, kdec_ref, decs_ref,
                masks_ref, cw_ref, cb_ref, lng_ref, lnb_ref, wout_ref, woutlo_ref, s0_ref, c0_ref,
                out_ref, snew_ref, cnew_ref, s_scr, ubuf, p_scr, cv_scr, cat_scr):
    step = pl.program_id(0)
    x = _load_rt(x_ref, TM) if x_is_rt else x_ref[...]
    xn = x * lax.rsqrt(jnp.mean(x * x, axis=-1, keepdims=True) + EPS) * g_ref[...]
    xn_parts = _split(xn) if precise else (xn.astype(BF16), None)

    def proj(lo, hi):
        out = jnp.dot(xn_parts[0], win_ref[:, lo:hi], preferred_element_type=F32)
        if precise:
            out = out + (jnp.dot(xn_parts[1], win_ref[:, lo:hi], preferred_element_type=F32)
                         + jnp.dot(xn_parts[0], winlo_ref[:, lo:hi], preferred_element_type=F32))
        return out

    q = proj(0, QK_W)
    k = proj(QK_W, 2 * QK_W)
    v = proj(2 * QK_W, 2 * QK_W + V_W)
    gate = proj(2 * QK_W + V_W, 2 * QK_W + 2 * V_W)
    ca = proj(2 * QK_W + 2 * V_W, 2 * QK_W + 2 * V_W + CONV_CH)
    cb = proj(2 * QK_W + 2 * V_W + CONV_CH, IN_COLS)

    lane = lax.broadcasted_iota(I32, (TM, QK_W), 1)
    first_half = (lane & (RET_DK // 2)) == 0

    def rope(t):
        partner = jnp.where(first_half, pltpu.roll(t, QK_W - RET_DK // 2, 1), pltpu.roll(t, RET_DK // 2, 1))
        return t * cos_ref[...] + partner * sin_ref[...]

    q = rope(q)
    k = rope(k) * (RET_DK ** -0.5)

    if carry:
        @pl.when(step % steps_per_seq == 0)
        def _():
            s_scr[...] = jnp.zeros_like(s_scr)
            ubuf[0:HIST, :] = jnp.zeros((HIST, CONV_CH), F32)
    else:
        s_scr[...] = jnp.zeros_like(s_scr)

    lane_c = lax.broadcasted_iota(I32, (c, QK_W), 1)
    for s in range(ns):
        r0 = s * c
        qc, kc, vc = q[r0:r0 + c], k[r0:r0 + c], v[r0:r0 + c]
        if not carry:
            for h in range(RET_HEADS):
                s_scr[h * RET_DK:(h + 1) * RET_DK, h * RET_DV:(h + 1) * RET_DV] = s0_ref[s, h]
        state = s_scr[...]
        o_cross = _mm(qc, state, _NN, precise) * qdec_ref[...]
        heads = []
        for h in range(RET_HEADS):
            qh = jnp.where((lane_c >> 6) == h, qc, 0.0)
            scores = _mm(qh, kc, _NT, precise) * intra_ref[h]
            heads.append(_mm(scores, vc[:, h * RET_DV:(h + 1) * RET_DV], _NN, precise))
        o = jnp.concatenate(heads, axis=1) + o_cross
        kv = _mm(kc * kdec_ref[...], vc, _TN, precise)
        s_new = decs_ref[...] * state + masks_ref[...] * kv
        if carry:
            s_scr[...] = s_new
            if s == ns - 1:
                @pl.when(step % steps_per_seq == steps_per_seq - 1)
                def _():
                    for h in range(RET_HEADS):
                        snew_ref[0, h] = s_new[h * RET_DK:(h + 1) * RET_DK, h * RET_DV:(h + 1) * RET_DV]
        else:
            for h in range(RET_HEADS):
                snew_ref[s, h] = s_new[h * RET_DK:(h + 1) * RET_DK, h * RET_DV:(h + 1) * RET_DV]

        normed = []
        for h in range(RET_HEADS):
            oh = o[:, h * RET_DV:(h + 1) * RET_DV]
            normed.append(oh * lax.rsqrt(jnp.mean(oh * oh, axis=-1, keepdims=True) + EPS))
        gc = gate[r0:r0 + c]
        og = jnp.concatenate(normed, axis=1) * (gc * jax.nn.sigmoid(gc))

        u = ca[r0:r0 + c] * jax.nn.sigmoid(cb[r0:r0 + c])
        if not carry:
            ubuf[0:HIST, :] = c0_ref[s]
        ubuf[HIST:HIST + c, :] = u
        off = HIST - (CONV_WIDTH - 1)
        for b in range(SUB):
            nrow = c + (2 * SUB if off + b + c - 1 >= c + SUB else SUB)
            for rb in range(0, nrow, 32):
                nr = min(32, nrow - rb)
                acc = None
                for j in range(b, CONV_WIDTH, SUB):
                    wj = jnp.concatenate([cw_ref[j * SUB:(j + 1) * SUB, :]] * (nr // SUB), axis=0)
                    term = ubuf[rb + j - b:rb + j - b + nr, :] * wj
                    acc = term if acc is None else acc + term
                p_scr[b, rb:rb + nr, :] = acc
        for rb in range(0, c, 32):
            acc = jnp.broadcast_to(cb_ref[...], (32, CONV_CH))
            for b in range(SUB):
                acc = acc + p_scr[b, rb + off + b:rb + off + b + 32, :]
            cv_scr[rb:rb + 32, :] = acc
        tail = ubuf[c:c + HIST, :]
        if carry:
            ubuf[0:HIST, :] = tail
            if s == ns - 1:
                @pl.when(step % steps_per_seq == steps_per_seq - 1)
                def _():
                    cnew_ref[0] = tail
        else:
            cnew_ref[s] = tail
        cv = cv_scr[0:c, :]
        mu = jnp.mean(cv, axis=-1, keepdims=True)
        xc = cv - mu
        var = jnp.mean(xc * xc, axis=-1, keepdims=True)
        cn = xc * lax.rsqrt(var + EPS) * lng_ref[...] + lnb_ref[...]
        cact = cn * jax.nn.sigmoid(cn)
        cat_scr[r0:r0 + c, 0:V_W] = og
        cat_scr[r0:r0 + c, V_W:D_MODEL] = cact

    y = _mm_w(cat_scr[...], wout_ref[...], woutlo_ref[...] if precise else None)
    _store_rt(out_ref, x + y, TM)


def _mixer_kernel(x_is_rt, precise, *refs):
    if x_is_rt:
        xp_ref = xs_ref = refs[0]
        refs = refs[1:]
    else:
        xp_ref, xs_ref = refs[0], refs[1]
        refs = refs[2:]
    if precise:
        winlo_ref, woutlo_ref = refs[0], refs[1]
        refs = refs[2:]
    else:
        winlo_ref = woutlo_ref = None
    (g_ref, win_ref, cw_ref, cb_ref, lng_ref, lnb_ref, wout_ref, masks_ref,
     cos_p, sin_p, intra_p, qdec_p, kdec_p, decs_p,
     cos_s, sin_s, intra_s, qdec_s, kdec_s, decs_s, s0_ref, c0_ref,
     out_ref, snew_p, cnew_p, snew_s, cnew_s, s_scr, ubuf, p_scr, cv_scr, cat_scr) = refs
    step = pl.program_id(0)

    @pl.when(step < N_PROMPT_STEPS)
    def _():
        _mixer_body(C_PROMPT, TM // C_PROMPT, True, SEQ // TM, x_is_rt, precise,
                    xp_ref, g_ref, win_ref, winlo_ref, cos_p, sin_p, intra_p, qdec_p, kdec_p, decs_p,
                    masks_ref, cw_ref, cb_ref, lng_ref, lnb_ref, wout_ref, woutlo_ref, None, None,
                    out_ref, snew_p, cnew_p, s_scr, ubuf, p_scr, cv_scr, cat_scr)

    @pl.when(step >= N_PROMPT_STEPS)
    def _():
        _mixer_body(DEC_SEQ, TM // DEC_SEQ, False, 1, x_is_rt, precise,
                    xs_ref, g_ref, win_ref, winlo_ref, cos_s, sin_s, intra_s, qdec_s, kdec_s, decs_s,
                    masks_ref, cw_ref, cb_ref, lng_ref, lnb_ref, wout_ref, woutlo_ref, s0_ref, c0_ref,
                    out_ref, snew_s, cnew_s, s_scr, ubuf, p_scr, cv_scr, cat_scr)


def _retention_tables(c, pos0):
    h = jnp.arange(RET_HEADS, dtype=F32)
    log_g = jnp.log1p(-jnp.exp2(-5.0 - h))
    idx = jnp.arange(c, dtype=F32)
    rel = idx[:, None] - idx[None, :]
    intra = jnp.where(rel >= 0, jnp.exp(log_g[:, None, None] * jnp.maximum(rel, 0.0)), 0.0)
    q_dec = jnp.exp(log_g[None, :] * (idx + 1.0)[:, None])
    k_dec = jnp.exp(log_g[None, :] * (c - 1.0 - idx)[:, None])
    chunk_dec = jnp.exp(log_g * c)
    qdec = jnp.repeat(q_dec, RET_DV, axis=1)
    kdec = jnp.repeat(k_dec, RET_DK, axis=1)
    row_h = jnp.arange(QK_W) // RET_DK
    col_h = jnp.arange(V_W) // RET_DV
    masks = (row_h[:, None] == col_h[None, :]).astype(F32)
    decs = masks * chunk_dec[row_h][:, None]
    return intra, qdec, kdec, decs, masks


def _rope_tables(pos):
    half = RET_DK // 2
    inv = ROPE_BASE ** (-jnp.arange(half, dtype=F32) / half)
    ang = pos.astype(F32)[:, None] * inv[None, :]
    cos, sin = jnp.cos(ang), jnp.sin(ang)
    cos_t = jnp.tile(jnp.concatenate([cos, cos], axis=1), (1, RET_HEADS))
    sin_t = jnp.tile(jnp.concatenate([-sin, sin], axis=1), (1, RET_HEADS))
    return cos_t, sin_t


def _mixer_layer(xs, g, win, cw, cb, lng, lnb, wout, s0, c0, tabs_p, tabs_s, lo_weights=()):
    precise = len(lo_weights) == 2
    lo_specs = [_resident((D_MODEL, IN_COLS)), _resident((D_MODEL, D_MODEL))] if precise else []
    x_is_rt = len(xs) == 1
    if x_is_rt:
        x_specs = [pl.BlockSpec((TM * SUB, LANES), lambda i: (i, 0))]
    else:
        x_specs = [pl.BlockSpec((TM, D_MODEL), lambda i: (jnp.minimum(i, N_PROMPT_STEPS - 1), 0)),
                   pl.BlockSpec((TM, D_MODEL), lambda i: (jnp.maximum(i - N_PROMPT_STEPS, 0), 0))]
    cos_p, sin_p, intra_p, qdec_p, kdec_p, decs_p, masks = tabs_p
    cos_s, sin_s, intra_s, qdec_s, kdec_s, decs_s, _ = tabs_s
    steps = SEQ // TM
    ns = TM // DEC_SEQ

    def table_specs(c):
        return [_const((RET_HEADS, c, c)), _const((c, V_W)), _const((c, QK_W)), _const((QK_W, V_W))]

    def sample_blk(i):
        return jnp.maximum(i - N_PROMPT_STEPS, 0)

    def prompt_blk(i):
        return jnp.minimum(i // steps, BATCH - 1)

    def prompt_pos(i):
        return (jnp.minimum(i, N_PROMPT_STEPS - 1) % steps, 0)

    return pl.pallas_call(
        functools.partial(_mixer_kernel, x_is_rt, precise),
        grid=(T_ALL // TM,),
        in_specs=x_specs + lo_specs + [_const((1, D_MODEL)), _resident((D_MODEL, IN_COLS)),
                  _const((HIST * SUB, CONV_CH)), _const((1, CONV_CH)), _const((1, CONV_CH)), _const((1, CONV_CH)),
                  _resident((D_MODEL, D_MODEL)), _const((QK_W, V_W)),
                  pl.BlockSpec((TM, QK_W), prompt_pos), pl.BlockSpec((TM, QK_W), prompt_pos)]
        + table_specs(C_PROMPT) + [_const((TM, QK_W)), _const((TM, QK_W))] + table_specs(DEC_SEQ)
        + [pl.BlockSpec((ns, RET_HEADS, RET_DK, RET_DV), lambda i: (sample_blk(i), 0, 0, 0)),
           pl.BlockSpec((ns, HIST, CONV_CH), lambda i: (sample_blk(i), 0, 0))],
        out_specs=[pl.BlockSpec((TM * SUB, LANES), lambda i: (i, 0)),
                   pl.BlockSpec((1, RET_HEADS, RET_DK, RET_DV), lambda i: (prompt_blk(i), 0, 0, 0)),
                   pl.BlockSpec((1, HIST, CONV_CH), lambda i: (prompt_blk(i), 0, 0)),
                   pl.BlockSpec((ns, RET_HEADS, RET_DK, RET_DV), lambda i: (sample_blk(i), 0, 0, 0)),
                   pl.BlockSpec((ns, HIST, CONV_CH), lambda i: (sample_blk(i), 0, 0))],
        out_shape=[jax.ShapeDtypeStruct((T_ALL * SUB, LANES), F32),
                   jax.ShapeDtypeStruct((BATCH, RET_HEADS, RET_DK, RET_DV), F32),
                   jax.ShapeDtypeStruct((BATCH, HIST, CONV_CH), F32),
                   jax.ShapeDtypeStruct((DEC_BATCH, RET_HEADS, RET_DK, RET_DV), F32),
                   jax.ShapeDtypeStruct((DEC_BATCH, HIST, CONV_CH), F32)],
        scratch_shapes=[pltpu.VMEM((QK_W, V_W), F32), pltpu.VMEM((HIST + C_PROMPT, CONV_CH), F32),
                        pltpu.VMEM((SUB, C_PROMPT + 2 * SUB, CONV_CH), F32),
                        pltpu.VMEM((C_PROMPT, CONV_CH), F32), pltpu.VMEM((TM, D_MODEL), F32)],
        compiler_params=_params(),
        name="mixer",
    )(*xs, *lo_weights, g, win, cw, cb, lng, lnb, wout, masks, cos_p, sin_p, intra_p, qdec_p, kdec_p, decs_p,
      cos_s, sin_s, intra_s, qdec_s, kdec_s, decs_s, s0, c0)


def _rmsnorm(x, g):
    return x * lax.rsqrt(jnp.mean(x * x, axis=-1, keepdims=True) + EPS) * g


def _router_kernel(h_ref, g_ref, wrhi_ref, wrlo_ref, br_ref, eid_ref, wts_ref):
    n = TR
    xn = _rmsnorm(_load_rt(h_ref, TR), g_ref[...])
    logits = (_mm_w(xn, wrhi_ref[...], wrlo_ref[...]) + br_ref[...]).T
    best = logits[0:1, :]
    gidx = jnp.zeros((1, n), I32)
    for r in range(1, N_GROUPS):
        row = logits[r:r + 1, :]
        upd = row > best
        gidx = jnp.where(upd, r, gidx)
        best = jnp.where(upd, row, best)
    denom = jnp.zeros((1, n), F32)
    for r in range(N_GROUPS):
        denom = denom + jnp.exp(logits[r:r + 1, :] - best)
    pgate = 1.0 / denom
    le = logits[EPG:2 * EPG, :]
    for gi in range(1, N_GROUPS):
        le = jnp.where(gidx == gi, logits[EPG * (gi + 1):EPG * (gi + 2), :], le)
    sub = lax.broadcasted_iota(I32, (EPG, n), 0)
    v1 = jnp.max(le, axis=0, keepdims=True)
    i1 = jnp.min(jnp.where(le == v1, sub, EPG), axis=0, keepdims=True)
    le2 = jnp.where(sub == i1, -jnp.inf, le)
    v2 = jnp.max(le2, axis=0, keepdims=True)
    i2 = jnp.min(jnp.where(le2 == v2, sub, EPG), axis=0, keepdims=True)
    e2 = jnp.exp(v2 - v1)
    den2 = 1.0 + e2
    wa = (1.0 / den2) * pgate
    wb = (e2 / den2) * pgate
    eid_ref[0:1, :] = gidx * EPG + i1
    eid_ref[1:2, :] = gidx * EPG + i2
    wts_ref[0:1, :] = wa
    wts_ref[1:2, :] = wb


def _plan_kernel(eid_ref, upper_ref, ltri_ref, lpos_ref, ntab_ref, lbtab_ref, gdtab_ref, te_ref, nt_ref,
                 padstart_ref, padlen_ref, cnt_scr, run_scr):
    phase = pl.program_id(0)
    i = pl.program_id(1)
    rows = lax.broadcasted_iota(I32, (N_EXPERTS, TR), 0)
    oh0 = (rows == eid_ref[0:1, :]).astype(F32)
    oh1 = (rows == eid_ref[1:2, :]).astype(F32)
    tot0 = jnp.sum(oh0, axis=1, keepdims=True)
    tot1 = jnp.sum(oh1, axis=1, keepdims=True)
    eye = (lax.broadcasted_iota(I32, (N_EXPERTS, LANES), 0)
           == lax.broadcasted_iota(I32, (N_EXPERTS, LANES), 1)).astype(F32)

    def to_row(col):
        return jnp.sum(col * eye, axis=0, keepdims=True).astype(I32)

    @pl.when((phase == 0) & (i == 0))
    def _():
        cnt_scr[...] = jnp.zeros_like(cnt_scr)

    @pl.when(phase == 0)
    def _():
        cnt_scr[...] = cnt_scr[...] + (tot0 + tot1)

    @pl.when((phase == 1) & (i == 0))
    def _():
        cnt = cnt_scr[...].astype(I32)
        ntile = ((cnt + (TE - 1)) >> (TE.bit_length() - 1)).astype(F32)
        incl = jnp.dot(ltri_ref[...], ntile, preferred_element_type=F32,
                       precision=lax.Precision.HIGHEST)
        run_scr[...] = (incl - ntile) * float(TE)
        ntot = incl[N_EXPERTS - 1:N_EXPERTS, 0:1]
        j = lax.broadcasted_iota(I32, (N_EXPERTS, 256), 1).astype(F32)
        j = jnp.minimum(j, ntot - 1.0)
        te = jnp.sum((incl[:, 0:1] <= j).astype(F32), axis=0, keepdims=True)
        te_ref[...] = te.astype(I32)
        nt_ref[...] = incl[N_EXPERTS - 1:N_EXPERTS, :].astype(I32)
        cntf = cnt_scr[:, 0:1]
        padstart_ref[...] = to_row((incl[:, 0:1] - ntile[:, 0:1]) * float(TE) + cntf)
        padlen_ref[...] = to_row(ntile[:, 0:1] * float(TE) - cntf)

    @pl.when(phase == 1)
    def _():
        run = run_scr[:, 0:1]
        tot = tot0 + tot1
        lbase = jnp.dot(ltri_ref[...], jnp.broadcast_to(tot, (N_EXPERTS, LANES)), preferred_element_type=F32,
                        precision=lax.Precision.HIGHEST)[:, 0:1] - tot
        ex0 = jnp.dot(oh0.astype(BF16), upper_ref[...], preferred_element_type=F32)
        ex1 = jnp.dot(oh1.astype(BF16), upper_ref[...], preferred_element_type=F32)
        p0 = jnp.sum(oh0 * (lbase + ex0), axis=0, keepdims=True)
        p1 = jnp.sum(oh1 * (lbase + tot0 + ex1), axis=0, keepdims=True)
        lpos_ref[0:1, :] = p0.astype(I32)
        lpos_ref[1:2, :] = p1.astype(I32)
        ntab_ref[0] = to_row(tot)
        lbtab_ref[0] = to_row(lbase)
        gdtab_ref[0] = to_row(run)
        run_scr[...] = run_scr[...] + tot


def _rows(start, n):
    return pl.ds(pl.multiple_of(start * SUB, SUB), n * SUB)


def _segment_copies(src_ref, src0, dst_ref, dst0, n, sem, wait):
    def piece(off, size):
        cp = pltpu.make_async_copy(src_ref.at[_rows(src0 + off, size)], dst_ref.at[_rows(dst0 + off, size)], sem)
        if wait:
            cp.wait()
        else:
            cp.start()

    nfull = n >> (SEG.bit_length() - 1)

    def full(j, carry):
        piece(j * SEG, SEG)
        return carry

    lax.fori_loop(0, nfull, full, 0)
    rem = n - nfull * SEG
    bit = SEG // 2
    while bit >= 1:
        @pl.when((rem & bit) != 0)
        def _(bit=bit):
            piece(nfull * SEG + (rem & ~(2 * bit - 1)), bit)
        bit //= 2


def _dispatch_kernel(ntab, lbtab, gdtab, padstart, padlen, nt, lpos_ref, h_ref, g_ref, xs_ref,
                     xn_scr, ls_scr, zero_scr, sems):
    b = pl.program_id(0)
    last = pl.num_programs(0) - 1
    slot = b % 2
    _store_rt(xn_scr, _rmsnorm(_load_rt(h_ref, TR), g_ref[...]), TR)

    def place(t, carry):
        row = xn_scr[_rows(t, 1), :]
        ls_scr[slot, _rows(lpos_ref[0, 0, t], 1), :] = row
        ls_scr[slot, _rows(lpos_ref[0, 1, t], 1), :] = row
        return carry

    lax.fori_loop(0, TR, place, 0, unroll=8)

    def per_expert(e, carry):
        _segment_copies(ls_scr.at[slot], lbtab[b, e], xs_ref, gdtab[b, e], ntab[b, e], sems.at[slot], False)
        return carry

    lax.fori_loop(0, N_EXPERTS, per_expert, 0)

    def wait_block(sl):
        pltpu.make_async_copy(ls_scr.at[sl], xs_ref.at[_rows(0, 2 * TR)], sems.at[sl]).wait()

    @pl.when(b > 0)
    def _():
        wait_block(1 - slot)

    @pl.when(b == last)
    def _():
        wait_block(slot)
        zero_scr[...] = jnp.zeros_like(zero_scr)
        for wait in (False, True):
            def per_expert(e, carry, wait=wait):
                _segment_copies(zero_scr, 0, xs_ref, padstart[e], padlen[e], sems.at[0], wait)
                return carry

            lax.fori_loop(0, N_EXPERTS, per_expert, 0)

            def per_tile(j, carry, wait=wait):
                cp = pltpu.make_async_copy(zero_scr, xs_ref.at[_rows(j * TE, TE)], sems.at[0])
                if wait:
                    cp.wait()
                else:
                    cp.start()
                return carry

            lax.fori_loop(nt[0], MAX_TILES, per_tile, 0)


def _ffn_kernel(te_ref, nt_ref, xs_ref, wg_ref, wu_ref, wd_ref, ys_ref, wg_scr, wu_scr, wd_scr):
    i = pl.program_id(0)
    first = (i == 0) | (te_ref[i] != te_ref[jnp.maximum(i - 1, 0)])

    @pl.when(first)
    def _():
        wg_scr[...] = wg_ref[0, 0].astype(BF16)
        wu_scr[...] = wu_ref[0, 0].astype(BF16)
        wd_scr[...] = wd_ref[0, 0].astype(BF16)

    @pl.when(i < nt_ref[0])
    def _():
        x = _load_rt(xs_ref, TE).astype(BF16)
        a = jnp.dot(x, wg_scr[...], preferred_element_type=F32)
        b = jnp.dot(x, wu_scr[...], preferred_element_type=F32)
        hdn = (a * jax.nn.sigmoid(a)) * b
        _store_rt(ys_ref, jnp.dot(hdn.astype(BF16), wd_scr[...], preferred_element_type=F32), TE)

    @pl.when(i >= nt_ref[0])
    def _():
        ys_ref[...] = jnp.zeros_like(ys_ref)


def _combine_kernel(final, ntab, lbtab, gdtab, lpos_ref, w_ref, h_ref, gf_ref, ys_ref, *rest):
    if final:
        yp_ref, ysamp_ref, yl_scr, o_scr, sems = rest
        dst = o_scr
    else:
        out_ref, yl_scr, sems = rest
        dst = out_ref
    b = pl.program_id(0)
    slot = b % 2

    def fetch(blk, sl):
        def per_expert(e, carry):
            _segment_copies(ys_ref, gdtab[blk, e], yl_scr.at[sl], lbtab[blk, e], ntab[blk, e], sems.at[sl], False)
            return carry

        lax.fori_loop(0, N_EXPERTS, per_expert, 0)

    @pl.when(b == 0)
    def _():
        fetch(0, 0)

    @pl.when(b + 1 < pl.num_programs(0))
    def _():
        fetch(b + 1, 1 - slot)

    pltpu.make_async_copy(ys_ref.at[_rows(0, 2 * TR)], yl_scr.at[slot], sems.at[slot]).wait()

    def merge(t, carry):
        ya = yl_scr[slot, _rows(lpos_ref[0, 0, t], 1), :]
        yb = yl_scr[slot, _rows(lpos_ref[0, 1, t], 1), :]
        dst[_rows(t, 1), :] = h_ref[_rows(t, 1), :] + (w_ref[0, 0, t] * ya + w_ref[0, 1, t] * yb)
        return carry

    lax.fori_loop(0, TR, merge, 0, unroll=8)
    if final:
        y = _rmsnorm(_load_rt(o_scr, TR), gf_ref[...])

        @pl.when(b < T_PROMPT // TR)
        def _():
            yp_ref[...] = y

        @pl.when(b >= T_PROMPT // TR)
        def _():
            ysamp_ref[...] = y


def _moe_layer(h_all, g, wr, br, layer, wg, wu, wd, gf, upper, ltri, final):
    n_r = T_ALL // TR
    h_blk = pl.BlockSpec((TR * SUB, LANES), lambda i, *_: (i, 0))
    eid, wts = pl.pallas_call(
        _router_kernel,
        grid=(n_r,),
        in_specs=[h_blk, _const((1, D_MODEL)), _const((D_MODEL, LANES)), _const((D_MODEL, LANES)),
                  _const((1, LANES))],
        out_specs=[pl.BlockSpec((2, TR), lambda i: (0, i)), pl.BlockSpec((2, TR), lambda i: (0, i))],
        out_shape=[jax.ShapeDtypeStruct((2, T_ALL), I32), jax.ShapeDtypeStruct((2, T_ALL), F32)],
        compiler_params=_params(),
        name="router",
    )(h_all, g, *wr, br)

    tab = jax.ShapeDtypeStruct((n_r, 1, LANES), I32)
    tab_blk = pl.BlockSpec((1, 1, LANES), lambda p, i: (i * p, 0, 0))
    lpos, ntab, lbtab, gdtab, te, nt, padstart, padlen = pl.pallas_call(
        _plan_kernel,
        grid=(2, n_r),
        in_specs=[pl.BlockSpec((2, TR), lambda p, i: (0, i)), _const((TR, TR)), _const((N_EXPERTS, N_EXPERTS))],
        out_specs=[pl.BlockSpec((2, TR), lambda p, i: (0, i * p)), tab_blk, tab_blk, tab_blk,
                   _const((1, 256)), _const((1, LANES)), _const((1, LANES)), _const((1, LANES))],
        out_shape=[jax.ShapeDtypeStruct((2, T_ALL), I32), tab, tab, tab, jax.ShapeDtypeStruct((1, 256), I32),
                   jax.ShapeDtypeStruct((1, LANES), I32), jax.ShapeDtypeStruct((1, LANES), I32),
                   jax.ShapeDtypeStruct((1, LANES), I32)],
        scratch_shapes=[pltpu.VMEM((N_EXPERTS, LANES), F32), pltpu.VMEM((N_EXPERTS, LANES), F32)],
        compiler_params=_params(2),
        name="plan",
    )(eid, upper, ltri)

    seg_tabs = (ntab.reshape(n_r, LANES), lbtab.reshape(n_r, LANES), gdtab.reshape(n_r, LANES))
    lpos3 = lpos.reshape(2, n_r, TR).transpose(1, 0, 2)
    wts3 = wts.reshape(2, n_r, TR).transpose(1, 0, 2)
    smem_blk = pl.BlockSpec((1, 2, TR), lambda i, *_: (i, 0, 0), memory_space=pltpu.SMEM)

    xs = pl.pallas_call(
        _dispatch_kernel,
        grid_spec=pltpu.PrefetchScalarGridSpec(
            num_scalar_prefetch=6,
            grid=(n_r,),
            in_specs=[smem_blk, h_blk, _const((1, D_MODEL))],
            out_specs=pl.BlockSpec(memory_space=pl.ANY),
            scratch_shapes=[pltpu.VMEM((TR * SUB, LANES), F32), pltpu.VMEM((2, 2 * TR * SUB, LANES), F32),
                            pltpu.VMEM((TE * SUB, LANES), F32), pltpu.SemaphoreType.DMA((2,))]),
        out_shape=jax.ShapeDtypeStruct((P_ROWS * SUB, LANES), F32),
        compiler_params=_params(),
        name="dispatch",
    )(*seg_tabs, padstart.reshape(LANES), padlen.reshape(LANES), nt[0, 0:1], lpos3, h_all, g)

    def row_map(i, te_ref, nt_ref):
        return (jnp.minimum(i, nt_ref[0] - 1), 0)

    ys = pl.pallas_call(
        _ffn_kernel,
        grid_spec=pltpu.PrefetchScalarGridSpec(
            num_scalar_prefetch=2,
            grid=(MAX_TILES,),
            in_specs=[pl.BlockSpec((TE * SUB, LANES), row_map),
                      pl.BlockSpec((1, 1, D_MODEL, EXPERT_FF), lambda i, te_ref, nt_ref: (layer, te_ref[i], 0, 0)),
                      pl.BlockSpec((1, 1, D_MODEL, EXPERT_FF), lambda i, te_ref, nt_ref: (layer, te_ref[i], 0, 0)),
                      pl.BlockSpec((1, 1, EXPERT_FF, D_MODEL), lambda i, te_ref, nt_ref: (layer, te_ref[i], 0, 0))],
            out_specs=pl.BlockSpec((TE * SUB, LANES), lambda i, te_ref, nt_ref: (i, 0)),
            scratch_shapes=[pltpu.VMEM((D_MODEL, EXPERT_FF), BF16), pltpu.VMEM((D_MODEL, EXPERT_FF), BF16),
                            pltpu.VMEM((EXPERT_FF, D_MODEL), BF16)]),
        out_shape=jax.ShapeDtypeStruct((P_ROWS * SUB, LANES), F32),
        compiler_params=_params(),
        name="expert_ffn",
    )(te.reshape(256), nt[0, 0:1], xs, wg, wu, wd)

    n_p = T_PROMPT // TR
    if final:
        out_specs = [pl.BlockSpec((TR, D_MODEL), lambda i, *_: (jnp.minimum(i, n_p - 1), 0)),
                     pl.BlockSpec((TR, D_MODEL), lambda i, *_: (jnp.maximum(i - n_p, 0), 0))]
        out_shape = [jax.ShapeDtypeStruct((T_PROMPT, D_MODEL), F32), jax.ShapeDtypeStruct((T_SAMPLE, D_MODEL), F32)]
        scratch = [pltpu.VMEM((2, 2 * TR * SUB, LANES), F32), pltpu.VMEM((TR * SUB, LANES), F32),
                   pltpu.SemaphoreType.DMA((2,))]
    else:
        out_specs = h_blk
        out_shape = jax.ShapeDtypeStruct((T_ALL * SUB, LANES), F32)
        scratch = [pltpu.VMEM((2, 2 * TR * SUB, LANES), F32), pltpu.SemaphoreType.DMA((2,))]
    return pl.pallas_call(
        functools.partial(_combine_kernel, final),
        grid_spec=pltpu.PrefetchScalarGridSpec(
            num_scalar_prefetch=3,
            grid=(n_r,),
            in_specs=[smem_blk, smem_blk, h_blk, _const((1, D_MODEL)), pl.BlockSpec(memory_space=pl.ANY)],
            out_specs=out_specs,
            scratch_shapes=scratch),
        out_shape=out_shape,
        compiler_params=_params(),
        name="combine",
    )(*seg_tabs, lpos3, wts3, h_all, gf, ys)


def kernel(x_prompt, x_sample, state_ret, cache_conv, norm_mix_g, w_in, conv_w, conv_b, conv_ln_g, conv_ln_b,
           w_out, norm_ffn_g, router_group_w, router_group_b, router_expert_w, router_expert_b, expert_w_gate,
           expert_w_up, expert_w_down, norm_final_g):
    h = (x_prompt.reshape(T_PROMPT, D_MODEL), x_sample.reshape(T_SAMPLE, D_MODEL))

    cos_p, sin_p = _rope_tables(jnp.arange(SEQ))
    cos_s, sin_s = _rope_tables(PAST_LEN + jnp.arange(DEC_SEQ))
    reps = TM // DEC_SEQ
    tabs_p = (cos_p, sin_p) + _retention_tables(C_PROMPT, 0)
    tabs_s = (jnp.tile(cos_s, (reps, 1)), jnp.tile(sin_s, (reps, 1))) + _retention_tables(DEC_SEQ, PAST_LEN)

    upper = (jnp.arange(TR)[:, None] < jnp.arange(TR)[None, :]).astype(BF16)
    ltri = (jnp.arange(N_EXPERTS)[:, None] >= jnp.arange(N_EXPERTS)[None, :]).astype(F32)
    pad_hist = HIST - (CONV_WIDTH - 1)
    gf = norm_final_g.reshape(1, D_MODEL)

    ret_p, conv_p, ret_s, conv_s = [], [], [], []
    for l in range(DEPTH):
        cw = jnp.repeat(jnp.pad(conv_w[l], ((0, HIST - CONV_WIDTH), (0, 0))), SUB, axis=0)
        c0 = jnp.pad(cache_conv[l], ((0, 0), (pad_hist, 0), (0, 0)))
        if l == 0:
            (win_hi, win_lo), (wout_hi, wout_lo) = _split(w_in[l]), _split(w_out[l])
            lo_weights = (win_lo, wout_lo)
        else:
            win_hi, wout_hi, lo_weights = w_in[l].astype(BF16), w_out[l].astype(BF16), ()
        wr = jnp.zeros((D_MODEL, LANES), F32)
        wr = wr.at[:, 0:N_GROUPS].set(router_group_w[l])
        wr = wr.at[:, EPG:EPG + N_EXPERTS].set(router_expert_w[l].transpose(1, 0, 2).reshape(D_MODEL, N_EXPERTS))
        br = jnp.zeros((1, LANES), F32)
        br = br.at[0, 0:N_GROUPS].set(router_group_b[l])
        br = br.at[0, EPG:EPG + N_EXPERTS].set(router_expert_b[l].reshape(N_EXPERTS))
        hm, sp, cp, ss, cs = _mixer_layer(
            h, norm_mix_g[l].reshape(1, D_MODEL), win_hi, cw, conv_b[l].reshape(1, CONV_CH),
            conv_ln_g[l].reshape(1, CONV_CH), conv_ln_b[l].reshape(1, CONV_CH), wout_hi,
            state_ret[l], c0, tabs_p, tabs_s, lo_weights)
        ret_p.append(sp)
        conv_p.append(cp[:, pad_hist:])
        ret_s.append(ss)
        conv_s.append(cs[:, pad_hist:])
        h = _moe_layer(hm, norm_ffn_g[l].reshape(1, D_MODEL), _split(wr), br, l, expert_w_gate, expert_w_up,
                       expert_w_down, gf, upper, ltri, final=(l == DEPTH - 1))
        if l < DEPTH - 1:
            h = (h,)

    y_prompt = h[0].reshape(BATCH, SEQ, D_MODEL)
    y_sample = h[1].reshape(DEC_BATCH, DEC_SEQ, D_MODEL)
    return (y_prompt, y_sample, jnp.stack(ret_p), jnp.stack(conv_p), jnp.stack(ret_s), jnp.stack(conv_s))
```

```python
import functools

import jax
import jax.numpy as jnp
from jax import lax
from jax.experimental import pallas as pl
from jax.experimental.pallas import tpu as pltpu

F32 = jnp.float32
BF16 = jnp.bfloat16
I32 = jnp.int32

D_MODEL = 1024
DEPTH = 2
BATCH, SEQ = 2, 8192
DEC_BATCH, DEC_SEQ = 16, 64
PAST_LEN = 1024
RET_HEADS, RET_DK, RET_DV = 4, 64, 128
QK_W = RET_HEADS * RET_DK
V_W = RET_HEADS * RET_DV
CONV_CH = 512
CONV_WIDTH = 31
HIST = 32
IN_COLS = 2 * QK_W + 2 * V_W + 2 * CONV_CH
N_GROUPS, EPG = 4, 8
N_EXPERTS = N_GROUPS * EPG
EXPERT_FF = 512
ROPE_BASE = 10000.0
EPS = 1e-6

T_PROMPT = BATCH * SEQ
T_SAMPLE = DEC_BATCH * DEC_SEQ
T_ALL = T_PROMPT + T_SAMPLE

SUB, LANES = 8, 128
TM = 512
C_PROMPT = 256
TR = 1024
TE = 512
SEG = 64
SMEM_ROWS = TR // LANES
N_SLOTS = 2 * T_ALL
MAX_TILES = N_SLOTS // TE + N_EXPERTS
P_ROWS = MAX_TILES * TE
VMEM_LIMIT = 60000 * 1024
N_PROMPT_STEPS = T_PROMPT // TM

SPLIT_FACTOR = float(2 ** 16 + 1)
_NN = (((1,), (0,)), ((), ()))
_NT = (((1,), (1,)), ((), ()))
_TN = (((0,), (0,)), ((), ()))


def _params(n_axes=1):
    return pltpu.CompilerParams(dimension_semantics=("arbitrary",) * n_axes,
                                vmem_limit_bytes=VMEM_LIMIT)


def _const(shape):
    nd = len(shape)
    return pl.BlockSpec(shape, lambda *_: (0,) * nd)


def _resident(shape):
    nd = len(shape)
    return pl.BlockSpec(shape, lambda *_: (0,) * nd, pipeline_mode=pl.Buffered(1))


def _load_rt(ref, rows):
    return jnp.concatenate([ref[pl.ds(s, rows, stride=SUB), :] for s in range(SUB)], axis=1)


def _store_rt(ref, val, rows):
    for s in range(SUB):
        ref[pl.ds(s, rows, stride=SUB), :] = val[:, s * LANES:(s + 1) * LANES]


def _split(a):
    t = a * SPLIT_FACTOR
    hi = t - (t - a)
    return hi.astype(BF16), (a - hi).astype(BF16)


def _mm(a, b, dims, precise):
    if not precise:
        return lax.dot_general(a.astype(BF16), b.astype(BF16), dims, preferred_element_type=F32)
    ah, al = _split(a)
    bh, bl = _split(b)
    return (lax.dot_general(ah, bh, dims, preferred_element_type=F32)
            + (lax.dot_general(al, bh, dims, preferred_element_type=F32)
               + lax.dot_general(ah, bl, dims, preferred_element_type=F32)))


def _mm_w(a, whi, wlo):
    if wlo is None:
        return jnp.dot(a.astype(BF16), whi, preferred_element_type=F32)
    ah, al = _split(a)
    return (jnp.dot(ah, whi, preferred_element_type=F32)
            + (jnp.dot(al, whi, preferred_element_type=F32) + jnp.dot(ah, wlo, preferred_element_type=F32)))


def _mixer_body(c, ns, carry, steps_per_seq, x_is_rt, precise,
                x_ref, g_ref, win_ref, winlo_ref, cos_ref, sin_ref, intra_ref, qdec_ref, kdec_ref, decs_ref,
                masks_ref, cw_ref, cb_ref, lng_ref, lnb_ref, wout_ref, woutlo_ref, s0_ref, c0_ref,
                out_ref, snew_ref, cnew_ref, s_scr, ubuf, p_scr, cv_scr, cat_scr):
    step = pl.program_id(0)
    x = _load_rt(x_ref, TM) if x_is_rt else x_ref[...]
    xn = x * lax.rsqrt(jnp.mean(x * x, axis=-1, keepdims=True) + EPS) * g_ref[...]
    xn_parts = _split(xn) if precise else (xn.astype(BF16), None)

    def proj(lo, hi):
        out = jnp.dot(xn_parts[0], win_ref[:, lo:hi], preferred_element_type=F32)
        if precise:
            out = out + (jnp.dot(xn_parts[1], win_ref[:, lo:hi], preferred_element_type=F32)
                         + jnp.dot(xn_parts[0], winlo_ref[:, lo:hi], preferred_element_type=F32))
        return out

    q = proj(0, QK_W)
    k = proj(QK_W, 2 * QK_W)
    v = proj(2 * QK_W, 2 * QK_W + V_W)
    gate = proj(2 * QK_W + V_W, 2 * QK_W + 2 * V_W)
    ca = proj(2 * QK_W + 2 * V_W, 2 * QK_W + 2 * V_W + CONV_CH)
    cb = proj(2 * QK_W + 2 * V_W + CONV_CH, IN_COLS)

    lane = lax.broadcasted_iota(I32, (TM, QK_W), 1)
    first_half = (lane & (RET_DK // 2)) == 0

    def rope(t):
        partner = jnp.where(first_half, pltpu.roll(t, QK_W - RET_DK // 2, 1), pltpu.roll(t, RET_DK // 2, 1))
        return t * cos_ref[...] + partner * sin_ref[...]

    q = rope(q)
    k = rope(k) * (RET_DK ** -0.5)

    if carry:
        @pl.when(step % steps_per_seq == 0)
        def _():
            s_scr[...] = jnp.zeros_like(s_scr)
            ubuf[0:HIST, :] = jnp.zeros((HIST, CONV_CH), F32)
    else:
        s_scr[...] = jnp.zeros_like(s_scr)

    lane_c = lax.broadcasted_iota(I32, (c, QK_W), 1)
    for s in range(ns):
        r0 = s * c
        qc, kc, vc = q[r0:r0 + c], k[r0:r0 + c], v[r0:r0 + c]
        if not carry:
            for h in range(RET_HEADS):
                s_scr[h * RET_DK:(h + 1) * RET_DK, h * RET_DV:(h + 1) * RET_DV] = s0_ref[s, h]
        state = s_scr[...]
        o_cross = _mm(qc, state, _NN, precise) * qdec_ref[...]
        heads = []
        for h in range(RET_HEADS):
            qh = jnp.where((lane_c >> 6) == h, qc, 0.0)
            scores = _mm(qh, kc, _NT, precise) * intra_ref[h]
            heads.append(_mm(scores, vc[:, h * RET_DV:(h + 1) * RET_DV], _NN, precise))
        o = jnp.concatenate(heads, axis=1) + o_cross
        kv = _mm(kc * kdec_ref[...], vc, _TN, precise)
        s_new = decs_ref[...] * state + masks_ref[...] * kv
        if carry:
            s_scr[...] = s_new
            if s == ns - 1:
                @pl.when(step % steps_per_seq == steps_per_seq - 1)
                def _():
                    for h in range(RET_HEADS):
                        snew_ref[0, h] = s_new[h * RET_DK:(h + 1) * RET_DK, h * RET_DV:(h + 1) * RET_DV]
        else:
            for h in range(RET_HEADS):
                snew_ref[s, h] = s_new[h * RET_DK:(h + 1) * RET_DK, h * RET_DV:(h + 1) * RET_DV]

        normed = []
        for h in range(RET_HEADS):
            oh = o[:, h * RET_DV:(h + 1) * RET_DV]
            normed.append(oh * lax.rsqrt(jnp.mean(oh * oh, axis=-1, keepdims=True) + EPS))
        gc = gate[r0:r0 + c]
        og = jnp.concatenate(normed, axis=1) * (gc * jax.nn.sigmoid(gc))

        u = ca[r0:r0 + c] * jax.nn.sigmoid(cb[r0:r0 + c])
        if not carry:
            ubuf[0:HIST, :] = c0_ref[s]
        ubuf[HIST:HIST + c, :] = u
        off = HIST - (CONV_WIDTH - 1)
        for b in range(SUB):
            nrow = c + (2 * SUB if off + b + c - 1 >= c + SUB else SUB)
            for rb in range(0, nrow, 32):
                nr = min(32, nrow - rb)
                acc = None
                for j in range(b, CONV_WIDTH, SUB):
                    wj = jnp.concatenate([cw_ref[j * SUB:(j + 1) * SUB, :]] * (nr // SUB), axis=0)
                    term = ubuf[rb + j - b:rb + j - b + nr, :] * wj
                    acc = term if acc is None else acc + term
                p_scr[b, rb:rb + nr, :] = acc
        for rb in range(0, c, 32):
            acc = jnp.broadcast_to(cb_ref[...], (32, CONV_CH))
            for b in range(SUB):
                acc = acc + p_scr[b, rb + off + b:rb + off + b + 32, :]
            cv_scr[rb:rb + 32, :] = acc
        tail = ubuf[c:c + HIST, :]
        if carry:
            ubuf[0:HIST, :] = tail
            if s == ns - 1:
                @pl.when(step % steps_per_seq == steps_per_seq - 1)
                def _():
                    cnew_ref[0] = tail
        else:
            cnew_ref[s] = tail
        cv = cv_scr[0:c, :]
        mu = jnp.mean(cv, axis=-1, keepdims=True)
        xc = cv - mu
        var = jnp.mean(xc * xc, axis=-1, keepdims=True)
        cn = xc * lax.rsqrt(var + EPS) * lng_ref[...] + lnb_ref[...]
        cact = cn * jax.nn.sigmoid(cn)
        cat_scr[r0:r0 + c, 0:V_W] = og
        cat_scr[r0:r0 + c, V_W:D_MODEL] = cact

    y = _mm_w(cat_scr[...], wout_ref[...], woutlo_ref[...] if precise else None)
    _store_rt(out_ref, x + y, TM)


def _mixer_kernel(x_is_rt, precise, *refs):
    if x_is_rt:
        xp_ref = xs_ref = refs[0]
        refs = refs[1:]
    else:
        xp_ref, xs_ref = refs[0], refs[1]
        refs = refs[2:]
    if precise:
        winlo_ref, woutlo_ref = refs[0], refs[1]
        refs = refs[2:]
    else:
        winlo_ref = woutlo_ref = None
    (g_ref, win_ref, cw_ref, cb_ref, lng_ref, lnb_ref, wout_ref, masks_ref,
     cos_p, sin_p, intra_p, qdec_p, kdec_p, decs_p,
     cos_s, sin_s, intra_s, qdec_s, kdec_s, decs_s, s0_ref, c0_ref,
     out_ref, snew_p, cnew_p, snew_s, cnew_s, s_scr, ubuf, p_scr, cv_scr, cat_scr) = refs
    step = pl.program_id(0)

    @pl.when(step < N_PROMPT_STEPS)
    def _():
        _mixer_body(C_PROMPT, TM // C_PROMPT, True, SEQ // TM, x_is_rt, precise,
                    xp_ref, g_ref, win_ref, winlo_ref, cos_p, sin_p, intra_p, qdec_p, kdec_p, decs_p,
                    masks_ref, cw_ref, cb_ref, lng_ref, lnb_ref, wout_ref, woutlo_ref, None, None,
                    out_ref, snew_p, cnew_p, s_scr, ubuf, p_scr, cv_scr, cat_scr)

    @pl.when(step >= N_PROMPT_STEPS)
    def _():
        _mixer_body(DEC_SEQ, TM // DEC_SEQ, False, 1, x_is_rt, precise,
                    xs_ref, g_ref, win_ref, winlo_ref, cos_s, sin_s, intra_s, qdec_s, kdec_s, decs_s,
                    masks_ref, cw_ref, cb_ref, lng_ref, lnb_ref, wout_ref, woutlo_ref, s0_ref, c0_ref,
                    out_ref, snew_s, cnew_s, s_scr, ubuf, p_scr, cv_scr, cat_scr)


def _retention_tables(c, pos0):
    h = jnp.arange(RET_HEADS, dtype=F32)
    log_g = jnp.log1p(-jnp.exp2(-5.0 - h))
    idx = jnp.arange(c, dtype=F32)
    rel = idx[:, None] - idx[None, :]
    intra = jnp.where(rel >= 0, jnp.exp(log_g[:, None, None] * jnp.maximum(rel, 0.0)), 0.0)
    q_dec = jnp.exp(log_g[None, :] * (idx + 1.0)[:, None])
    k_dec = jnp.exp(log_g[None, :] * (c - 1.0 - idx)[:, None])
    chunk_dec = jnp.exp(log_g * c)
    qdec = jnp.repeat(q_dec, RET_DV, axis=1)
    kdec = jnp.repeat(k_dec, RET_DK, axis=1)
    row_h = jnp.arange(QK_W) // RET_DK
    col_h = jnp.arange(V_W) // RET_DV
    masks = (row_h[:, None] == col_h[None, :]).astype(F32)
    decs = masks * chunk_dec[row_h][:, None]
    return intra, qdec, kdec, decs, masks


def _rope_tables(pos):
    half = RET_DK // 2
    inv = ROPE_BASE ** (-jnp.arange(half, dtype=F32) / half)
    ang = pos.astype(F32)[:, None] * inv[None, :]
    cos, sin = jnp.cos(ang), jnp.sin(ang)
    cos_t = jnp.tile(jnp.concatenate([cos, cos], axis=1), (1, RET_HEADS))
    sin_t = jnp.tile(jnp.concatenate([-sin, sin], axis=1), (1, RET_HEADS))
    return cos_t, sin_t


def _mixer_layer(xs, g, win, cw, cb, lng, lnb, wout, s0, c0, tabs_p, tabs_s, lo_weights=()):
    precise = len(lo_weights) == 2
    lo_specs = [_resident((D_MODEL, IN_COLS)), _resident((D_MODEL, D_MODEL))] if precise else []
    x_is_rt = len(xs) == 1
    if x_is_rt:
        x_specs = [pl.BlockSpec((TM * SUB, LANES), lambda i: (i, 0))]
    else:
        x_specs = [pl.BlockSpec((TM, D_MODEL), lambda i: (jnp.minimum(i, N_PROMPT_STEPS - 1), 0)),
                   pl.BlockSpec((TM, D_MODEL), lambda i: (jnp.maximum(i - N_PROMPT_STEPS, 0), 0))]
    cos_p, sin_p, intra_p, qdec_p, kdec_p, decs_p, masks = tabs_p
    cos_s, sin_s, intra_s, qdec_s, kdec_s, decs_s, _ = tabs_s
    steps = SEQ // TM
    ns = TM // DEC_SEQ

    def table_specs(c):
        return [_const((RET_HEADS, c, c)), _const((c, V_W)), _const((c, QK_W)), _const((QK_W, V_W))]

    def sample_blk(i):
        return jnp.maximum(i - N_PROMPT_STEPS, 0)

    def prompt_blk(i):
        return jnp.minimum(i // steps, BATCH - 1)

    def prompt_pos(i):
        return (jnp.minimum(i, N_PROMPT_STEPS - 1) % steps, 0)

    return pl.pallas_call(
        functools.partial(_mixer_kernel, x_is_rt, precise),
        grid=(T_ALL // TM,),
        in_specs=x_specs + lo_specs + [_const((1, D_MODEL)), _resident((D_MODEL, IN_COLS)),
                  _const((HIST * SUB, CONV_CH)), _const((1, CONV_CH)), _const((1, CONV_CH)), _const((1, CONV_CH)),
                  _resident((D_MODEL, D_MODEL)), _const((QK_W, V_W)),
                  pl.BlockSpec((TM, QK_W), prompt_pos), pl.BlockSpec((TM, QK_W), prompt_pos)]
        + table_specs(C_PROMPT) + [_const((TM, QK_W)), _const((TM, QK_W))] + table_specs(DEC_SEQ)
        + [pl.BlockSpec((ns, RET_HEADS, RET_DK, RET_DV), lambda i: (sample_blk(i), 0, 0, 0)),
           pl.BlockSpec((ns, HIST, CONV_CH), lambda i: (sample_blk(i), 0, 0))],
        out_specs=[pl.BlockSpec((TM * SUB, LANES), lambda i: (i, 0)),
                   pl.BlockSpec((1, RET_HEADS, RET_DK, RET_DV), lambda i: (prompt_blk(i), 0, 0, 0)),
                   pl.BlockSpec((1, HIST, CONV_CH), lambda i: (prompt_blk(i), 0, 0)),
                   pl.BlockSpec((ns, RET_HEADS, RET_DK, RET_DV), lambda i: (sample_blk(i), 0, 0, 0)),
                   pl.BlockSpec((ns, HIST, CONV_CH), lambda i: (sample_blk(i), 0, 0))],
        out_shape=[jax.ShapeDtypeStruct((T_ALL * SUB, LANES), F32),
                   jax.ShapeDtypeStruct((BATCH, RET_HEADS, RET_DK, RET_DV), F32),
                   jax.ShapeDtypeStruct((BATCH, HIST, CONV_CH), F32),
                   jax.ShapeDtypeStruct((DEC_BATCH, RET_HEADS, RET_DK, RET_DV), F32),
                   jax.ShapeDtypeStruct((DEC_BATCH, HIST, CONV_CH), F32)],
        scratch_shapes=[pltpu.VMEM((QK_W, V_W), F32), pltpu.VMEM((HIST + C_PROMPT, CONV_CH), F32),
                        pltpu.VMEM((SUB, C_PROMPT + 2 * SUB, CONV_CH), F32),
                        pltpu.VMEM((C_PROMPT, CONV_CH), F32), pltpu.VMEM((TM, D_MODEL), F32)],
        compiler_params=_params(),
        name="mixer",
    )(*xs, *lo_weights, g, win, cw, cb, lng, lnb, wout, masks, cos_p, sin_p, intra_p, qdec_p, kdec_p, decs_p,
      cos_s, sin_s, intra_s, qdec_s, kdec_s, decs_s, s0, c0)


def _rmsnorm(x, g):
    return x * lax.rsqrt(jnp.mean(x * x, axis=-1, keepdims=True) + EPS) * g


def _router_kernel(h_ref, g_ref, wrhi_ref, wrlo_ref, br_ref, eid_ref, wts_ref):
    n = TR
    xn = _rmsnorm(_load_rt(h_ref, TR), g_ref[...])
    logits = (_mm_w(xn, wrhi_ref[...], wrlo_ref[...]) + br_ref[...]).T
    best = logits[0:1, :]
    gidx = jnp.zeros((1, n), I32)
    for r in range(1, N_GROUPS):
        row = logits[r:r + 1, :]
        upd = row > best
        gidx = jnp.where(upd, r, gidx)
        best = jnp.where(upd, row, best)
    denom = jnp.zeros((1, n), F32)
    for r in range(N_GROUPS):
        denom = denom + jnp.exp(logits[r:r + 1, :] - best)
    pgate = 1.0 / denom
    le = logits[EPG:2 * EPG, :]
    for gi in range(1, N_GROUPS):
        le = jnp.where(gidx == gi, logits[EPG * (gi + 1):EPG * (gi + 2), :], le)
    sub = lax.broadcasted_iota(I32, (EPG, n), 0)
    v1 = jnp.max(le, axis=0, keepdims=True)
    i1 = jnp.min(jnp.where(le == v1, sub, EPG), axis=0, keepdims=True)
    le2 = jnp.where(sub == i1, -jnp.inf, le)
    v2 = jnp.max(le2, axis=0, keepdims=True)
    i2 = jnp.min(jnp.where(le2 == v2, sub, EPG), axis=0, keepdims=True)
    e2 = jnp.exp(v2 - v1)
    den2 = 1.0 + e2
    wa = (1.0 / den2) * pgate
    wb = (e2 / den2) * pgate
    eid_ref[0:1, :] = gidx * EPG + i1
    eid_ref[1:2, :] = gidx * EPG + i2
    wts_ref[0:1, :] = wa
    wts_ref[1:2, :] = wb


def _plan_kernel(eid_ref, upper_ref, ltri_ref, lpos_ref, ntab_ref, lbtab_ref, gdtab_ref, te_ref, nt_ref,
                 padstart_ref, padlen_ref, cnt_scr, run_scr):
    phase = pl.program_id(0)
    i = pl.program_id(1)
    rows = lax.broadcasted_iota(I32, (N_EXPERTS, TR), 0)
    oh0 = (rows == eid_ref[0:1, :]).astype(F32)
    oh1 = (rows == eid_ref[1:2, :]).astype(F32)
    tot0 = jnp.sum(oh0, axis=1, keepdims=True)
    tot1 = jnp.sum(oh1, axis=1, keepdims=True)
    eye = (lax.broadcasted_iota(I32, (N_EXPERTS, LANES), 0)
           == lax.broadcasted_iota(I32, (N_EXPERTS, LANES), 1)).astype(F32)

    def to_row(col):
        return jnp.sum(col * eye, axis=0, keepdims=True).astype(I32)

    @pl.when((phase == 0) & (i == 0))
    def _():
        cnt_scr[...] = jnp.zeros_like(cnt_scr)

    @pl.when(phase == 0)
    def _():
        cnt_scr[...] = cnt_scr[...] + (tot0 + tot1)

    @pl.when((phase == 1) & (i == 0))
    def _():
        cnt = cnt_scr[...].astype(I32)
        ntile = ((cnt + (TE - 1)) >> (TE.bit_length() - 1)).astype(F32)
        incl = jnp.dot(ltri_ref[...], ntile, preferred_element_type=F32,
                       precision=lax.Precision.HIGHEST)
        run_scr[...] = (incl - ntile) * float(TE)
        ntot = incl[N_EXPERTS - 1:N_EXPERTS, 0:1]
        j = lax.broadcasted_iota(I32, (N_EXPERTS, 256), 1).astype(F32)
        j = jnp.minimum(j, ntot - 1.0)
        te = jnp.sum((incl[:, 0:1] <= j).astype(F32), axis=0, keepdims=True)
        te_ref[...] = te.astype(I32)
        nt_ref[...] = incl[N_EXPERTS - 1:N_EXPERTS, :].astype(I32)
        cntf = cnt_scr[:, 0:1]
        padstart_ref[...] = to_row((incl[:, 0:1] - ntile[:, 0:1]) * float(TE) + cntf)
        padlen_ref[...] = to_row(ntile[:, 0:1] * float(TE) - cntf)

    @pl.when(phase == 1)
    def _():
        run = run_scr[:, 0:1]
        tot = tot0 + tot1
        lbase = jnp.dot(ltri_ref[...], jnp.broadcast_to(tot, (N_EXPERTS, LANES)), preferred_element_type=F32,
                        precision=lax.Precision.HIGHEST)[:, 0:1] - tot
        ex0 = jnp.dot(oh0.astype(BF16), upper_ref[...], preferred_element_type=F32)
        ex1 = jnp.dot(oh1.astype(BF16), upper_ref[...], preferred_element_type=F32)
        p0 = jnp.sum(oh0 * (lbase + ex0), axis=0, keepdims=True)
        p1 = jnp.sum(oh1 * (lbase + tot0 + ex1), axis=0, keepdims=True)
        lpos_ref[0:1, :] = p0.astype(I32)
        lpos_ref[1:2, :] = p1.astype(I32)
        ntab_ref[0] = to_row(tot)
        lbtab_ref[0] = to_row(lbase)
        gdtab_ref[0] = to_row(run)
        run_scr[...] = run_scr[...] + tot


def _rows(start, n):
    return pl.ds(pl.multiple_of(start * SUB, SUB), n * SUB)


def _segment_copies(src_ref, src0, dst_ref, dst0, n, sem, wait):
    def piece(off, size):
        cp = pltpu.make_async_copy(src_ref.at[_rows(src0 + off, size)], dst_ref.at[_rows(dst0 + off, size)], sem)
        if wait:
            cp.wait()
        else:
            cp.start()

    nfull = n >> (SEG.bit_length() - 1)

    def full(j, carry):
        piece(j * SEG, SEG)
        return carry

    lax.fori_loop(0, nfull, full, 0)
    rem = n - nfull * SEG
    bit = SEG // 2
    while bit >= 1:
        @pl.when((rem & bit) != 0)
        def _(bit=bit):
            piece(nfull * SEG + (rem & ~(2 * bit - 1)), bit)
        bit //= 2


def _dispatch_kernel(ntab, lbtab, gdtab, padstart, padlen, nt, lpos_ref, h_ref, g_ref, xs_ref,
                     xn_scr, ls_scr, zero_scr, sems):
    b = pl.program_id(0)
    last = pl.num_programs(0) - 1
    slot = b % 2
    _store_rt(xn_scr, _rmsnorm(_load_rt(h_ref, TR), g_ref[...]), TR)

    def place(grp, carry):
        for lane in range(LANES):
            row = xn_scr[_rows(grp * LANES + lane, 1), :]
            for k in range(2):
                dst = pl.multiple_of(lpos_ref[0, k * SMEM_ROWS + grp, lane], SUB)
                ls_scr[slot, pl.ds(dst, SUB), :] = row
        return carry

    lax.fori_loop(0, SMEM_ROWS, place, 0)

    def per_expert(e, carry):
        _segment_copies(ls_scr.at[slot], lbtab[b, e], xs_ref, gdtab[b, e], ntab[b, e], sems.at[slot], False)
        return carry

    lax.fori_loop(0, N_EXPERTS, per_expert, 0)

    def wait_block(sl):
        pltpu.make_async_copy(ls_scr.at[sl], xs_ref.at[_rows(0, 2 * TR)], sems.at[sl]).wait()

    @pl.when(b > 0)
    def _():
        wait_block(1 - slot)

    @pl.when(b == last)
    def _():
        wait_block(slot)
        zero_scr[...] = jnp.zeros_like(zero_scr)
        for wait in (False, True):
            def per_expert(e, carry, wait=wait):
                _segment_copies(zero_scr, 0, xs_ref, padstart[e], padlen[e], sems.at[0], wait)
                return carry

            lax.fori_loop(0, N_EXPERTS, per_expert, 0)

            def per_tile(j, carry, wait=wait):
                cp = pltpu.make_async_copy(zero_scr, xs_ref.at[_rows(j * TE, TE)], sems.at[0])
                if wait:
                    cp.wait()
                else:
                    cp.start()
                return carry

            lax.fori_loop(nt[0], MAX_TILES, per_tile, 0)


def _ffn_kernel(te_ref, nt_ref, xs_ref, wg_ref, wu_ref, wd_ref, ys_ref, wg_scr, wu_scr, wd_scr):
    i = pl.program_id(0)
    first = (i == 0) | (te_ref[i] != te_ref[jnp.maximum(i - 1, 0)])

    @pl.when(first)
    def _():
        wg_scr[...] = wg_ref[0, 0].astype(BF16)
        wu_scr[...] = wu_ref[0, 0].astype(BF16)
        wd_scr[...] = wd_ref[0, 0].astype(BF16)

    @pl.when(i < nt_ref[0])
    def _():
        x = _load_rt(xs_ref, TE).astype(BF16)
        a = jnp.dot(x, wg_scr[...], preferred_element_type=F32)
        b = jnp.dot(x, wu_scr[...], preferred_element_type=F32)
        hdn = (a * jax.nn.sigmoid(a)) * b
        _store_rt(ys_ref, jnp.dot(hdn.astype(BF16), wd_scr[...], preferred_element_type=F32), TE)

    @pl.when(i >= nt_ref[0])
    def _():
        ys_ref[...] = jnp.zeros_like(ys_ref)


def _combine_kernel(final, ntab, lbtab, gdtab, lpos_ref, w_ref, h_ref, gf_ref, ys_ref, *rest):
    if final:
        yp_ref, ysamp_ref, yl_scr, o_scr, sems = rest
        dst = o_scr
    else:
        out_ref, yl_scr, sems = rest
        dst = out_ref
    b = pl.program_id(0)
    slot = b % 2

    def fetch(blk, sl):
        def per_expert(e, carry):
            _segment_copies(ys_ref, gdtab[blk, e], yl_scr.at[sl], lbtab[blk, e], ntab[blk, e], sems.at[sl], False)
            return carry

        lax.fori_loop(0, N_EXPERTS, per_expert, 0)

    @pl.when(b == 0)
    def _():
        fetch(0, 0)

    @pl.when(b + 1 < pl.num_programs(0))
    def _():
        fetch(b + 1, 1 - slot)

    pltpu.make_async_copy(ys_ref.at[_rows(0, 2 * TR)], yl_scr.at[slot], sems.at[slot]).wait()

    def merge(grp, carry):
        for lane in range(LANES):
            ya = yl_scr[slot, pl.ds(pl.multiple_of(lpos_ref[0, grp, lane], SUB), SUB), :]
            yb = yl_scr[slot, pl.ds(pl.multiple_of(lpos_ref[0, SMEM_ROWS + grp, lane], SUB), SUB), :]
            wa, wb = w_ref[0, grp, lane], w_ref[0, SMEM_ROWS + grp, lane]
            rows = _rows(grp * LANES + lane, 1)
            dst[rows, :] = h_ref[rows, :] + (wa * ya + wb * yb)
        return carry

    lax.fori_loop(0, SMEM_ROWS, merge, 0)
    if final:
        y = _rmsnorm(_load_rt(o_scr, TR), gf_ref[...])

        @pl.when(b < T_PROMPT // TR)
        def _():
            yp_ref[...] = y

        @pl.when(b >= T_PROMPT // TR)
        def _():
            ysamp_ref[...] = y


def _moe_layer(h_all, g, wr, br, layer, wg, wu, wd, gf, upper, ltri, final):
    n_r = T_ALL // TR
    h_blk = pl.BlockSpec((TR * SUB, LANES), lambda i, *_: (i, 0))
    eid, wts = pl.pallas_call(
        _router_kernel,
        grid=(n_r,),
        in_specs=[h_blk, _const((1, D_MODEL)), _const((D_MODEL, LANES)), _const((D_MODEL, LANES)),
                  _const((1, LANES))],
        out_specs=[pl.BlockSpec((2, TR), lambda i: (0, i)), pl.BlockSpec((2, TR), lambda i: (0, i))],
        out_shape=[jax.ShapeDtypeStruct((2, T_ALL), I32), jax.ShapeDtypeStruct((2, T_ALL), F32)],
        compiler_params=_params(),
        name="router",
    )(h_all, g, *wr, br)

    tab = jax.ShapeDtypeStruct((n_r, 1, LANES), I32)
    tab_blk = pl.BlockSpec((1, 1, LANES), lambda p, i: (i * p, 0, 0))
    lpos, ntab, lbtab, gdtab, te, nt, padstart, padlen = pl.pallas_call(
        _plan_kernel,
        grid=(2, n_r),
        in_specs=[pl.BlockSpec((2, TR), lambda p, i: (0, i)), _const((TR, TR)), _const((N_EXPERTS, N_EXPERTS))],
        out_specs=[pl.BlockSpec((2, TR), lambda p, i: (0, i * p)), tab_blk, tab_blk, tab_blk,
                   _const((1, 256)), _const((1, LANES)), _const((1, LANES)), _const((1, LANES))],
        out_shape=[jax.ShapeDtypeStruct((2, T_ALL), I32), tab, tab, tab, jax.ShapeDtypeStruct((1, 256), I32),
                   jax.ShapeDtypeStruct((1, LANES), I32), jax.ShapeDtypeStruct((1, LANES), I32),
                   jax.ShapeDtypeStruct((1, LANES), I32)],
        scratch_shapes=[pltpu.VMEM((N_EXPERTS, LANES), F32), pltpu.VMEM((N_EXPERTS, LANES), F32)],
        compiler_params=_params(2),
        name="plan",
    )(eid, upper, ltri)

    seg_tabs = (ntab.reshape(n_r, LANES), lbtab.reshape(n_r, LANES), gdtab.reshape(n_r, LANES))
    def per_block(a):
        return a.reshape(2, n_r, SMEM_ROWS, LANES).transpose(1, 0, 2, 3).reshape(n_r, 2 * SMEM_ROWS, LANES)

    lpos3 = per_block(lpos * SUB)
    wts3 = per_block(wts)
    smem_blk = pl.BlockSpec((1, 2 * SMEM_ROWS, LANES), lambda i, *_: (i, 0, 0), memory_space=pltpu.SMEM)

    xs = pl.pallas_call(
        _dispatch_kernel,
        grid_spec=pltpu.PrefetchScalarGridSpec(
            num_scalar_prefetch=6,
            grid=(n_r,),
            in_specs=[smem_blk, h_blk, _const((1, D_MODEL))],
            out_specs=pl.BlockSpec(memory_space=pl.ANY),
            scratch_shapes=[pltpu.VMEM((TR * SUB, LANES), F32), pltpu.VMEM((2, 2 * TR * SUB, LANES), F32),
                            pltpu.VMEM((TE * SUB, LANES), F32), pltpu.SemaphoreType.DMA((2,))]),
        out_shape=jax.ShapeDtypeStruct((P_ROWS * SUB, LANES), F32),
        compiler_params=_params(),
        name="dispatch",
    )(*seg_tabs, padstart.reshape(LANES), padlen.reshape(LANES), nt[0, 0:1], lpos3, h_all, g)

    def row_map(i, te_ref, nt_ref):
        return (jnp.minimum(i, nt_ref[0] - 1), 0)

    ys = pl.pallas_call(
        _ffn_kernel,
        grid_spec=pltpu.PrefetchScalarGridSpec(
            num_scalar_prefetch=2,
            grid=(MAX_TILES,),
            in_specs=[pl.BlockSpec((TE * SUB, LANES), row_map),
                      pl.BlockSpec((1, 1, D_MODEL, EXPERT_FF), lambda i, te_ref, nt_ref: (layer, te_ref[i], 0, 0)),
                      pl.BlockSpec((1, 1, D_MODEL, EXPERT_FF), lambda i, te_ref, nt_ref: (layer, te_ref[i], 0, 0)),
                      pl.BlockSpec((1, 1, EXPERT_FF, D_MODEL), lambda i, te_ref, nt_ref: (layer, te_ref[i], 0, 0))],
            out_specs=pl.BlockSpec((TE * SUB, LANES), lambda i, te_ref, nt_ref: (i, 0)),
            scratch_shapes=[pltpu.VMEM((D_MODEL, EXPERT_FF), BF16), pltpu.VMEM((D_MODEL, EXPERT_FF), BF16),
                            pltpu.VMEM((EXPERT_FF, D_MODEL), BF16)]),
        out_shape=jax.ShapeDtypeStruct((P_ROWS * SUB, LANES), F32),
        compiler_params=_params(),
        name="expert_ffn",
    )(te.reshape(256), nt[0, 0:1], xs, wg, wu, wd)

    n_p = T_PROMPT // TR
    if final:
        out_specs = [pl.BlockSpec((TR, D_MODEL), lambda i, *_: (jnp.minimum(i, n_p - 1), 0)),
                     pl.BlockSpec((TR, D_MODEL), lambda i, *_: (jnp.maximum(i - n_p, 0), 0))]
        out_shape = [jax.ShapeDtypeStruct((T_PROMPT, D_MODEL), F32), jax.ShapeDtypeStruct((T_SAMPLE, D_MODEL), F32)]
        scratch = [pltpu.VMEM((2, 2 * TR * SUB, LANES), F32), pltpu.VMEM((TR * SUB, LANES), F32),
                   pltpu.SemaphoreType.DMA((2,))]
    else:
        out_specs = h_blk
        out_shape = jax.ShapeDtypeStruct((T_ALL * SUB, LANES), F32)
        scratch = [pltpu.VMEM((2, 2 * TR * SUB, LANES), F32), pltpu.SemaphoreType.DMA((2,))]
    return pl.pallas_call(
        functools.partial(_combine_kernel, final),
        grid_spec=pltpu.PrefetchScalarGridSpec(
            num_scalar_prefetch=3,
            grid=(n_r,),
            in_specs=[smem_blk, smem_blk, h_blk, _const((1, D_MODEL)), pl.BlockSpec(memory_space=pl.ANY)],
            out_specs=out_specs,
            scratch_shapes=scratch),
        out_shape=out_shape,
        compiler_params=_params(),
        name="combine",
    )(*seg_tabs, lpos3, wts3, h_all, gf, ys)


def kernel(x_prompt, x_sample, state_ret, cache_conv, norm_mix_g, w_in, conv_w, conv_b, conv_ln_g, conv_ln_b,
           w_out, norm_ffn_g, router_group_w, router_group_b, router_expert_w, router_expert_b, expert_w_gate,
           expert_w_up, expert_w_down, norm_final_g):
    h = (x_prompt.reshape(T_PROMPT, D_MODEL), x_sample.reshape(T_SAMPLE, D_MODEL))

    cos_p, sin_p = _rope_tables(jnp.arange(SEQ))
    cos_s, sin_s = _rope_tables(PAST_LEN + jnp.arange(DEC_SEQ))
    reps = TM // DEC_SEQ
    tabs_p = (cos_p, sin_p) + _retention_tables(C_PROMPT, 0)
    tabs_s = (jnp.tile(cos_s, (reps, 1)), jnp.tile(sin_s, (reps, 1))) + _retention_tables(DEC_SEQ, PAST_LEN)

    upper = (jnp.arange(TR)[:, None] < jnp.arange(TR)[None, :]).astype(BF16)
    ltri = (jnp.arange(N_EXPERTS)[:, None] >= jnp.arange(N_EXPERTS)[None, :]).astype(F32)
    pad_hist = HIST - (CONV_WIDTH - 1)
    gf = norm_final_g.reshape(1, D_MODEL)

    ret_p, conv_p, ret_s, conv_s = [], [], [], []
    for l in range(DEPTH):
        cw = jnp.repeat(jnp.pad(conv_w[l], ((0, HIST - CONV_WIDTH), (0, 0))), SUB, axis=0)
        c0 = jnp.pad(cache_conv[l], ((0, 0), (pad_hist, 0), (0, 0)))
        if l == 0:
            (win_hi, win_lo), (wout_hi, wout_lo) = _split(w_in[l]), _split(w_out[l])
            lo_weights = (win_lo, wout_lo)
        else:
            win_hi, wout_hi, lo_weights = w_in[l].astype(BF16), w_out[l].astype(BF16), ()
        wr = jnp.zeros((D_MODEL, LANES), F32)
        wr = wr.at[:, 0:N_GROUPS].set(router_group_w[l])
        wr = wr.at[:, EPG:EPG + N_EXPERTS].set(router_expert_w[l].transpose(1, 0, 2).reshape(D_MODEL, N_EXPERTS))
        br = jnp.zeros((1, LANES), F32)
        br = br.at[0, 0:N_GROUPS].set(router_group_b[l])
        br = br.at[0, EPG:EPG + N_EXPERTS].set(router_expert_b[l].reshape(N_EXPERTS))
        hm, sp, cp, ss, cs = _mixer_layer(
            h, norm_mix_g[l].reshape(1, D_MODEL), win_hi, cw, conv_b[l].reshape(1, CONV_CH),
            conv_ln_g[l].reshape(1, CONV_CH), conv_ln_b[l].reshape(1, CONV_CH), wout_hi,
            state_ret[l], c0, tabs_p, tabs_s, lo_weights)
        ret_p.append(sp)
        conv_p.append(cp[:, pad_hist:])
        ret_s.append(ss)
        conv_s.append(cs[:, pad_hist:])
        h = _moe_layer(hm, norm_ffn_g[l].reshape(1, D_MODEL), _split(wr), br, l, expert_w_gate, expert_w_up,
                       expert_w_down, gf, upper, ltri, final=(l == DEPTH - 1))
        if l < DEPTH - 1:
            h = (h,)

    y_prompt = h[0].reshape(BATCH, SEQ, D_MODEL)
    y_sample = h[1].reshape(DEC_BATCH, DEC_SEQ, D_MODEL)
    return (y_prompt, y_sample, jnp.stack(ret_p), jnp.stack(conv_p), jnp.stack(ret_s), jnp.stack(conv_s))
```

```python
import functools

import jax
import jax.numpy as jnp
from jax import lax
from jax.experimental import pallas as pl
from jax.experimental.pallas import tpu as pltpu

F32 = jnp.float32
BF16 = jnp.bfloat16
I32 = jnp.int32

D_MODEL = 1024
DEPTH = 2
BATCH, SEQ = 2, 8192
DEC_BATCH, DEC_SEQ = 16, 64
PAST_LEN = 1024
RET_HEADS, RET_DK, RET_DV = 4, 64, 128
QK_W = RET_HEADS * RET_DK
V_W = RET_HEADS * RET_DV
CONV_CH = 512
CONV_WIDTH = 31
HIST = 32
IN_COLS = 2 * QK_W + 2 * V_W + 2 * CONV_CH
N_GROUPS, EPG = 4, 8
N_EXPERTS = N_GROUPS * EPG
EXPERT_FF = 512
ROPE_BASE = 10000.0
EPS = 1e-6

T_PROMPT = BATCH * SEQ
T_SAMPLE = DEC_BATCH * DEC_SEQ
T_ALL = T_PROMPT + T_SAMPLE

SUB, LANES = 8, 128
TM = 512
C_PROMPT = 256
TR = 1024
TE = 512
SEG = 64
SMEM_ROWS = TR // LANES
N_SLOTS = 2 * T_ALL
MAX_TILES = N_SLOTS // TE + N_EXPERTS
P_ROWS = MAX_TILES * TE
VMEM_LIMIT = 60000 * 1024
N_PROMPT_STEPS = T_PROMPT // TM

SPLIT_FACTOR = float(2 ** 16 + 1)
_NN = (((1,), (0,)), ((), ()))
_NT = (((1,), (1,)), ((), ()))
_TN = (((0,), (0,)), ((), ()))


def _params(n_axes=1):
    return pltpu.CompilerParams(dimension_semantics=("arbitrary",) * n_axes,
                                vmem_limit_bytes=VMEM_LIMIT)


def _const(shape):
    nd = len(shape)
    return pl.BlockSpec(shape, lambda *_: (0,) * nd)


def _resident(shape):
    nd = len(shape)
    return pl.BlockSpec(shape, lambda *_: (0,) * nd, pipeline_mode=pl.Buffered(1))


def _load_rt(ref, rows):
    return jnp.concatenate([ref[pl.ds(s, rows, stride=SUB), :] for s in range(SUB)], axis=1)


def _store_rt(ref, val, rows):
    for s in range(SUB):
        ref[pl.ds(s, rows, stride=SUB), :] = val[:, s * LANES:(s + 1) * LANES]


def _split(a):
    t = a * SPLIT_FACTOR
    hi = t - (t - a)
    return hi.astype(BF16), (a - hi).astype(BF16)


def _mm(a, b, dims, precise):
    if not precise:
        return lax.dot_general(a.astype(BF16), b.astype(BF16), dims, preferred_element_type=F32)
    ah, al = _split(a)
    bh, bl = _split(b)
    return (lax.dot_general(ah, bh, dims, preferred_element_type=F32)
            + (lax.dot_general(al, bh, dims, preferred_element_type=F32)
               + lax.dot_general(ah, bl, dims, preferred_element_type=F32)))


def _mm_w(a, whi, wlo):
    if wlo is None:
        return jnp.dot(a.astype(BF16), whi, preferred_element_type=F32)
    ah, al = _split(a)
    return (jnp.dot(ah, whi, preferred_element_type=F32)
            + (jnp.dot(al, whi, preferred_element_type=F32) + jnp.dot(ah, wlo, preferred_element_type=F32)))


def _mixer_body(c, ns, carry, steps_per_seq, x_is_rt, precise,
                x_ref, g_ref, win_ref, winlo_ref, cos_ref, sin_ref, intra_ref, qdec_ref, kdec_ref, decs_ref,
                masks_ref, cw_ref, cb_ref, lng_ref, lnb_ref, wout_ref, woutlo_ref, s0_ref, c0_ref,
                out_ref, snew_ref, cnew_ref, s_scr, ubuf, p_scr, cv_scr, cat_scr):
    step = pl.program_id(0)
    x = _load_rt(x_ref, TM) if x_is_rt else x_ref[...]
    xn = x * lax.rsqrt(jnp.mean(x * x, axis=-1, keepdims=True) + EPS) * g_ref[...]
    xn_parts = _split(xn) if precise else (xn.astype(BF16), None)

    def proj(lo, hi):
        out = jnp.dot(xn_parts[0], win_ref[:, lo:hi], preferred_element_type=F32)
        if precise:
            out = out + (jnp.dot(xn_parts[1], win_ref[:, lo:hi], preferred_element_type=F32)
                         + jnp.dot(xn_parts[0], winlo_ref[:, lo:hi], preferred_element_type=F32))
        return out

    q = proj(0, QK_W)
    k = proj(QK_W, 2 * QK_W)
    v = proj(2 * QK_W, 2 * QK_W + V_W)
    gate = proj(2 * QK_W + V_W, 2 * QK_W + 2 * V_W)
    ca = proj(2 * QK_W + 2 * V_W, 2 * QK_W + 2 * V_W + CONV_CH)
    cb = proj(2 * QK_W + 2 * V_W + CONV_CH, IN_COLS)

    lane = lax.broadcasted_iota(I32, (TM, QK_W), 1)
    first_half = (lane & (RET_DK // 2)) == 0

    def rope(t):
        partner = jnp.where(first_half, pltpu.roll(t, QK_W - RET_DK // 2, 1), pltpu.roll(t, RET_DK // 2, 1))
        return t * cos_ref[...] + partner * sin_ref[...]

    q = rope(q)
    k = rope(k) * (RET_DK ** -0.5)

    if carry:
        @pl.when(step % steps_per_seq == 0)
        def _():
            s_scr[...] = jnp.zeros_like(s_scr)
            ubuf[0:HIST, :] = jnp.zeros((HIST, CONV_CH), F32)
    else:
        s_scr[...] = jnp.zeros_like(s_scr)

    lane_c = lax.broadcasted_iota(I32, (c, QK_W), 1)
    for s in range(ns):
        r0 = s * c
        qc, kc, vc = q[r0:r0 + c], k[r0:r0 + c], v[r0:r0 + c]
        if not carry:
            for h in range(RET_HEADS):
                s_scr[h * RET_DK:(h + 1) * RET_DK, h * RET_DV:(h + 1) * RET_DV] = s0_ref[s, h]
        state = s_scr[...]
        o_cross = _mm(qc, state, _NN, precise) * qdec_ref[...]
        heads = []
        for h in range(RET_HEADS):
            qh = jnp.where((lane_c >> 6) == h, qc, 0.0)
            scores = _mm(qh, kc, _NT, precise) * intra_ref[h]
            heads.append(_mm(scores, vc[:, h * RET_DV:(h + 1) * RET_DV], _NN, precise))
        o = jnp.concatenate(heads, axis=1) + o_cross
        kv = _mm(kc * kdec_ref[...], vc, _TN, precise)
        s_new = decs_ref[...] * state + masks_ref[...] * kv
        if carry:
            s_scr[...] = s_new
            if s == ns - 1:
                @pl.when(step % steps_per_seq == steps_per_seq - 1)
                def _():
                    for h in range(RET_HEADS):
                        snew_ref[0, h] = s_new[h * RET_DK:(h + 1) * RET_DK, h * RET_DV:(h + 1) * RET_DV]
        else:
            for h in range(RET_HEADS):
                snew_ref[s, h] = s_new[h * RET_DK:(h + 1) * RET_DK, h * RET_DV:(h + 1) * RET_DV]

        normed = []
        for h in range(RET_HEADS):
            oh = o[:, h * RET_DV:(h + 1) * RET_DV]
            normed.append(oh * lax.rsqrt(jnp.mean(oh * oh, axis=-1, keepdims=True) + EPS))
        gc = gate[r0:r0 + c]
        og = jnp.concatenate(normed, axis=1) * (gc * jax.nn.sigmoid(gc))

        u = ca[r0:r0 + c] * jax.nn.sigmoid(cb[r0:r0 + c])
        if not carry:
            ubuf[0:HIST, :] = c0_ref[s]
        ubuf[HIST:HIST + c, :] = u
        off = HIST - (CONV_WIDTH - 1)
        for b in range(SUB):
            nrow = c + (2 * SUB if off + b + c - 1 >= c + SUB else SUB)
            for rb in range(0, nrow, 32):
                nr = min(32, nrow - rb)
                acc = None
                for j in range(b, CONV_WIDTH, SUB):
                    wj = jnp.concatenate([cw_ref[j * SUB:(j + 1) * SUB, :]] * (nr // SUB), axis=0)
                    term = ubuf[rb + j - b:rb + j - b + nr, :] * wj
                    acc = term if acc is None else acc + term
                p_scr[b, rb:rb + nr, :] = acc
        for rb in range(0, c, 32):
            acc = jnp.broadcast_to(cb_ref[...], (32, CONV_CH))
            for b in range(SUB):
                acc = acc + p_scr[b, rb + off + b:rb + off + b + 32, :]
            cv_scr[rb:rb + 32, :] = acc
        tail = ubuf[c:c + HIST, :]
        if carry:
            ubuf[0:HIST, :] = tail
            if s == ns - 1:
                @pl.when(step % steps_per_seq == steps_per_seq - 1)
                def _():
                    cnew_ref[0] = tail
        else:
            cnew_ref[s] = tail
        cv = cv_scr[0:c, :]
        mu = jnp.mean(cv, axis=-1, keepdims=True)
        xc = cv - mu
        var = jnp.mean(xc * xc, axis=-1, keepdims=True)
        cn = xc * lax.rsqrt(var + EPS) * lng_ref[...] + lnb_ref[...]
        cact = cn * jax.nn.sigmoid(cn)
        cat_scr[r0:r0 + c, 0:V_W] = og.astype(cat_scr.dtype)
        cat_scr[r0:r0 + c, V_W:D_MODEL] = cact.astype(cat_scr.dtype)

    y = _mm_w(cat_scr[...], wout_ref[...], woutlo_ref[...] if precise else None)
    _store_rt(out_ref, x + y, TM)


def _mixer_kernel(x_is_rt, precise, *refs):
    if x_is_rt:
        xp_ref = xs_ref = refs[0]
        refs = refs[1:]
    else:
        xp_ref, xs_ref = refs[0], refs[1]
        refs = refs[2:]
    if precise:
        winlo_ref, woutlo_ref = refs[0], refs[1]
        refs = refs[2:]
    else:
        winlo_ref = woutlo_ref = None
    (g_ref, win_ref, cw_ref, cb_ref, lng_ref, lnb_ref, wout_ref, masks_ref,
     cos_p, sin_p, intra_p, qdec_p, kdec_p, decs_p,
     cos_s, sin_s, intra_s, qdec_s, kdec_s, decs_s, s0_ref, c0_ref,
     out_ref, snew_p, cnew_p, snew_s, cnew_s, s_scr, ubuf, p_scr, cv_scr, cat_scr) = refs
    step = pl.program_id(0)

    @pl.when(step < N_PROMPT_STEPS)
    def _():
        _mixer_body(C_PROMPT, TM // C_PROMPT, True, SEQ // TM, x_is_rt, precise,
                    xp_ref, g_ref, win_ref, winlo_ref, cos_p, sin_p, intra_p, qdec_p, kdec_p, decs_p,
                    masks_ref, cw_ref, cb_ref, lng_ref, lnb_ref, wout_ref, woutlo_ref, None, None,
                    out_ref, snew_p, cnew_p, s_scr, ubuf, p_scr, cv_scr, cat_scr)

    @pl.when(step >= N_PROMPT_STEPS)
    def _():
        _mixer_body(DEC_SEQ, TM // DEC_SEQ, False, 1, x_is_rt, precise,
                    xs_ref, g_ref, win_ref, winlo_ref, cos_s, sin_s, intra_s, qdec_s, kdec_s, decs_s,
                    masks_ref, cw_ref, cb_ref, lng_ref, lnb_ref, wout_ref, woutlo_ref, s0_ref, c0_ref,
                    out_ref, snew_s, cnew_s, s_scr, ubuf, p_scr, cv_scr, cat_scr)


def _retention_tables(c, pos0):
    h = jnp.arange(RET_HEADS, dtype=F32)
    log_g = jnp.log1p(-jnp.exp2(-5.0 - h))
    idx = jnp.arange(c, dtype=F32)
    rel = idx[:, None] - idx[None, :]
    intra = jnp.where(rel >= 0, jnp.exp(log_g[:, None, None] * jnp.maximum(rel, 0.0)), 0.0)
    q_dec = jnp.exp(log_g[None, :] * (idx + 1.0)[:, None])
    k_dec = jnp.exp(log_g[None, :] * (c - 1.0 - idx)[:, None])
    chunk_dec = jnp.exp(log_g * c)
    qdec = jnp.repeat(q_dec, RET_DV, axis=1)
    kdec = jnp.repeat(k_dec, RET_DK, axis=1)
    row_h = jnp.arange(QK_W) // RET_DK
    col_h = jnp.arange(V_W) // RET_DV
    masks = (row_h[:, None] == col_h[None, :]).astype(F32)
    decs = masks * chunk_dec[row_h][:, None]
    return intra, qdec, kdec, decs, masks


def _rope_tables(pos):
    half = RET_DK // 2
    inv = ROPE_BASE ** (-jnp.arange(half, dtype=F32) / half)
    ang = pos.astype(F32)[:, None] * inv[None, :]
    cos, sin = jnp.cos(ang), jnp.sin(ang)
    cos_t = jnp.tile(jnp.concatenate([cos, cos], axis=1), (1, RET_HEADS))
    sin_t = jnp.tile(jnp.concatenate([-sin, sin], axis=1), (1, RET_HEADS))
    return cos_t, sin_t


def _mixer_layer(xs, g, win, cw, cb, lng, lnb, wout, s0, c0, tabs_p, tabs_s, lo_weights=()):
    precise = len(lo_weights) == 2
    lo_specs = [_resident((D_MODEL, IN_COLS)), _resident((D_MODEL, D_MODEL))] if precise else []
    x_is_rt = len(xs) == 1
    if x_is_rt:
        x_specs = [pl.BlockSpec((TM * SUB, LANES), lambda i: (i, 0))]
    else:
        x_specs = [pl.BlockSpec((TM, D_MODEL), lambda i: (jnp.minimum(i, N_PROMPT_STEPS - 1), 0)),
                   pl.BlockSpec((TM, D_MODEL), lambda i: (jnp.maximum(i - N_PROMPT_STEPS, 0), 0))]
    cos_p, sin_p, intra_p, qdec_p, kdec_p, decs_p, masks = tabs_p
    cos_s, sin_s, intra_s, qdec_s, kdec_s, decs_s, _ = tabs_s
    steps = SEQ // TM
    ns = TM // DEC_SEQ

    def table_specs(c):
        return [_const((RET_HEADS, c, c)), _const((c, V_W)), _const((c, QK_W)), _const((QK_W, V_W))]

    def sample_blk(i):
        return jnp.maximum(i - N_PROMPT_STEPS, 0)

    def prompt_blk(i):
        return jnp.minimum(i // steps, BATCH - 1)

    def prompt_pos(i):
        return (jnp.minimum(i, N_PROMPT_STEPS - 1) % steps, 0)

    return pl.pallas_call(
        functools.partial(_mixer_kernel, x_is_rt, precise),
        grid=(T_ALL // TM,),
        in_specs=x_specs + lo_specs + [_const((1, D_MODEL)), _resident((D_MODEL, IN_COLS)),
                  _const((HIST * SUB, CONV_CH)), _const((1, CONV_CH)), _const((1, CONV_CH)), _const((1, CONV_CH)),
                  _resident((D_MODEL, D_MODEL)), _const((QK_W, V_W)),
                  pl.BlockSpec((TM, QK_W), prompt_pos), pl.BlockSpec((TM, QK_W), prompt_pos)]
        + table_specs(C_PROMPT) + [_const((TM, QK_W)), _const((TM, QK_W))] + table_specs(DEC_SEQ)
        + [pl.BlockSpec((ns, RET_HEADS, RET_DK, RET_DV), lambda i: (sample_blk(i), 0, 0, 0)),
           pl.BlockSpec((ns, HIST, CONV_CH), lambda i: (sample_blk(i), 0, 0))],
        out_specs=[pl.BlockSpec((TM * SUB, LANES), lambda i: (i, 0)),
                   pl.BlockSpec((1, RET_HEADS, RET_DK, RET_DV), lambda i: (prompt_blk(i), 0, 0, 0)),
                   pl.BlockSpec((1, HIST, CONV_CH), lambda i: (prompt_blk(i), 0, 0)),
                   pl.BlockSpec((ns, RET_HEADS, RET_DK, RET_DV), lambda i: (sample_blk(i), 0, 0, 0)),
                   pl.BlockSpec((ns, HIST, CONV_CH), lambda i: (sample_blk(i), 0, 0))],
        out_shape=[jax.ShapeDtypeStruct((T_ALL * SUB, LANES), F32),
                   jax.ShapeDtypeStruct((BATCH, RET_HEADS, RET_DK, RET_DV), F32),
                   jax.ShapeDtypeStruct((BATCH, HIST, CONV_CH), F32),
                   jax.ShapeDtypeStruct((DEC_BATCH, RET_HEADS, RET_DK, RET_DV), F32),
                   jax.ShapeDtypeStruct((DEC_BATCH, HIST, CONV_CH), F32)],
        scratch_shapes=[pltpu.VMEM((QK_W, V_W), F32), pltpu.VMEM((HIST + C_PROMPT, CONV_CH), F32),
                        pltpu.VMEM((SUB, C_PROMPT + 2 * SUB, CONV_CH), F32),
                        pltpu.VMEM((C_PROMPT, CONV_CH), F32),
                        pltpu.VMEM((TM, D_MODEL), F32 if precise else BF16)],
        compiler_params=_params(),
        name="mixer",
    )(*xs, *lo_weights, g, win, cw, cb, lng, lnb, wout, masks, cos_p, sin_p, intra_p, qdec_p, kdec_p, decs_p,
      cos_s, sin_s, intra_s, qdec_s, kdec_s, decs_s, s0, c0)


def _rmsnorm(x, g):
    return x * lax.rsqrt(jnp.mean(x * x, axis=-1, keepdims=True) + EPS) * g


def _router_kernel(h_ref, g_ref, wrhi_ref, wrlo_ref, br_ref, eid_ref, wts_ref, cnt_ref):
    n = TR
    xn = _rmsnorm(_load_rt(h_ref, TR), g_ref[...])
    logits = (_mm_w(xn, wrhi_ref[...], wrlo_ref[...]) + br_ref[...]).T
    best = logits[0:1, :]
    gidx = jnp.zeros((1, n), I32)
    for r in range(1, N_GROUPS):
        row = logits[r:r + 1, :]
        upd = row > best
        gidx = jnp.where(upd, r, gidx)
        best = jnp.where(upd, row, best)
    denom = jnp.zeros((1, n), F32)
    for r in range(N_GROUPS):
        denom = denom + jnp.exp(logits[r:r + 1, :] - best)
    pgate = 1.0 / denom
    le = logits[EPG:2 * EPG, :]
    for gi in range(1, N_GROUPS):
        le = jnp.where(gidx == gi, logits[EPG * (gi + 1):EPG * (gi + 2), :], le)
    sub = lax.broadcasted_iota(I32, (EPG, n), 0)
    v1 = jnp.max(le, axis=0, keepdims=True)
    i1 = jnp.min(jnp.where(le == v1, sub, EPG), axis=0, keepdims=True)
    le2 = jnp.where(sub == i1, -jnp.inf, le)
    v2 = jnp.max(le2, axis=0, keepdims=True)
    i2 = jnp.min(jnp.where(le2 == v2, sub, EPG), axis=0, keepdims=True)
    e2 = jnp.exp(v2 - v1)
    den2 = 1.0 + e2
    wa = (1.0 / den2) * pgate
    wb = (e2 / den2) * pgate
    e1 = gidx * EPG + i1
    e2 = gidx * EPG + i2
    eid_ref[0:1, :] = e1
    eid_ref[1:2, :] = e2
    wts_ref[0:1, :] = wa
    wts_ref[1:2, :] = wb
    experts = lax.broadcasted_iota(I32, (N_EXPERTS, n), 0)
    hits = (experts == e1).astype(F32) + (experts == e2).astype(F32)

    @pl.when(pl.program_id(0) == 0)
    def _():
        cnt_ref[...] = jnp.zeros_like(cnt_ref)

    cnt_ref[...] = cnt_ref[...] + jnp.sum(hits, axis=1, keepdims=True)


def _plan_kernel(eid_ref, cnt_ref, upper_ref, ltri_ref, lpos_ref, ntab_ref, lbtab_ref, gdtab_ref, te_ref, nt_ref,
                 padstart_ref, padlen_ref, run_scr):
    i = pl.program_id(0)
    rows = lax.broadcasted_iota(I32, (N_EXPERTS, TR), 0)
    oh0 = (rows == eid_ref[0:1, :]).astype(F32)
    oh1 = (rows == eid_ref[1:2, :]).astype(F32)
    tot0 = jnp.sum(oh0, axis=1, keepdims=True)
    tot1 = jnp.sum(oh1, axis=1, keepdims=True)
    eye = (lax.broadcasted_iota(I32, (N_EXPERTS, LANES), 0)
           == lax.broadcasted_iota(I32, (N_EXPERTS, LANES), 1)).astype(F32)

    def to_row(col):
        return jnp.sum(col * eye, axis=0, keepdims=True).astype(I32)

    @pl.when(i == 0)
    def _():
        cnt = cnt_ref[...].astype(I32)
        ntile = ((cnt + (TE - 1)) >> (TE.bit_length() - 1)).astype(F32)
        incl = jnp.dot(ltri_ref[...], ntile, preferred_element_type=F32,
                       precision=lax.Precision.HIGHEST)
        run_scr[...] = (incl - ntile) * float(TE)
        ntot = incl[N_EXPERTS - 1:N_EXPERTS, 0:1]
        j = lax.broadcasted_iota(I32, (N_EXPERTS, 256), 1).astype(F32)
        j = jnp.minimum(j, ntot - 1.0)
        te = jnp.sum((incl[:, 0:1] <= j).astype(F32), axis=0, keepdims=True)
        te_ref[...] = te.astype(I32)
        nt_ref[...] = incl[N_EXPERTS - 1:N_EXPERTS, :].astype(I32)
        cntf = cnt_ref[:, 0:1]
        padstart_ref[...] = to_row((incl[:, 0:1] - ntile[:, 0:1]) * float(TE) + cntf)
        padlen_ref[...] = to_row(ntile[:, 0:1] * float(TE) - cntf)

    run = run_scr[:, 0:1]
    tot = tot0 + tot1
    lbase = jnp.dot(ltri_ref[...], jnp.broadcast_to(tot, (N_EXPERTS, LANES)), preferred_element_type=F32,
                    precision=lax.Precision.HIGHEST)[:, 0:1] - tot
    ex0 = jnp.dot(oh0.astype(BF16), upper_ref[...], preferred_element_type=F32)
    ex1 = jnp.dot(oh1.astype(BF16), upper_ref[...], preferred_element_type=F32)
    p0 = jnp.sum(oh0 * (lbase + ex0), axis=0, keepdims=True)
    p1 = jnp.sum(oh1 * (lbase + tot0 + ex1), axis=0, keepdims=True)
    lpos_ref[0:1, :] = p0.astype(I32)
    lpos_ref[1:2, :] = p1.astype(I32)
    ntab_ref[0] = to_row(tot)
    lbtab_ref[0] = to_row(lbase)
    gdtab_ref[0] = to_row(run)
    run_scr[...] = run_scr[...] + tot


def _rows(start, n):
    return pl.ds(pl.multiple_of(start * SUB, SUB), n * SUB)


def _segment_copies(src_ref, src0, dst_ref, dst0, n, sem, wait):
    def piece(off, size):
        cp = pltpu.make_async_copy(src_ref.at[_rows(src0 + off, size)], dst_ref.at[_rows(dst0 + off, size)], sem)
        if wait:
            cp.wait()
        else:
            cp.start()

    nfull = n >> (SEG.bit_length() - 1)

    def full(j, carry):
        piece(j * SEG, SEG)
        return carry

    lax.fori_loop(0, nfull, full, 0)
    rem = n - nfull * SEG
    bit = SEG // 2
    while bit >= 1:
        @pl.when((rem & bit) != 0)
        def _(bit=bit):
            piece(nfull * SEG + (rem & ~(2 * bit - 1)), bit)
        bit //= 2


def _dispatch_kernel(ntab, lbtab, gdtab, padstart, padlen, nt, lpos_ref, h_ref, g_ref, xs_ref,
                     xn_scr, ls_scr, zero_scr, sems):
    b = pl.program_id(0)
    last = pl.num_programs(0) - 1
    slot = b % 2
    _store_rt(xn_scr, _rmsnorm(_load_rt(h_ref, TR), g_ref[...]), TR)

    def place(grp, carry):
        for lane in range(LANES):
            row = xn_scr[_rows(grp * LANES + lane, 1), :]
            for k in range(2):
                dst = pl.multiple_of(lpos_ref[0, k * SMEM_ROWS + grp, lane], SUB)
                ls_scr[slot, pl.ds(dst, SUB), :] = row
        return carry

    lax.fori_loop(0, SMEM_ROWS, place, 0)

    def per_expert(e, carry):
        _segment_copies(ls_scr.at[slot], lbtab[b, e], xs_ref, gdtab[b, e], ntab[b, e], sems.at[slot], False)
        return carry

    lax.fori_loop(0, N_EXPERTS, per_expert, 0)

    def wait_block(sl):
        pltpu.make_async_copy(ls_scr.at[sl], xs_ref.at[_rows(0, 2 * TR)], sems.at[sl]).wait()

    @pl.when(b > 0)
    def _():
        wait_block(1 - slot)

    @pl.when(b == last)
    def _():
        wait_block(slot)
        zero_scr[...] = jnp.zeros_like(zero_scr)
        for wait in (False, True):
            def per_expert(e, carry, wait=wait):
                _segment_copies(zero_scr, 0, xs_ref, padstart[e], padlen[e], sems.at[0], wait)
                return carry

            lax.fori_loop(0, N_EXPERTS, per_expert, 0)

            def per_tile(j, carry, wait=wait):
                cp = pltpu.make_async_copy(zero_scr, xs_ref.at[_rows(j * TE, TE)], sems.at[0])
                if wait:
                    cp.wait()
                else:
                    cp.start()
                return carry

            lax.fori_loop(nt[0], MAX_TILES, per_tile, 0)


def _ffn_kernel(te_ref, nt_ref, xs_ref, wg_ref, wu_ref, wd_ref, ys_ref, wg_scr, wu_scr, wd_scr):
    i = pl.program_id(0)
    first = (i == 0) | (te_ref[i] != te_ref[jnp.maximum(i - 1, 0)])

    @pl.when(first)
    def _():
        wg_scr[...] = wg_ref[0, 0].astype(BF16)
        wu_scr[...] = wu_ref[0, 0].astype(BF16)
        wd_scr[...] = wd_ref[0, 0].astype(BF16)

    @pl.when(i < nt_ref[0])
    def _():
        x = _load_rt(xs_ref, TE).astype(BF16)
        a = jnp.dot(x, wg_scr[...], preferred_element_type=F32)
        b = jnp.dot(x, wu_scr[...], preferred_element_type=F32)
        hdn = (a * jax.nn.sigmoid(a)) * b
        _store_rt(ys_ref, jnp.dot(hdn.astype(BF16), wd_scr[...], preferred_element_type=F32), TE)

    @pl.when(i >= nt_ref[0])
    def _():
        ys_ref[...] = jnp.zeros_like(ys_ref)


def _combine_kernel(final, ntab, lbtab, gdtab, lpos_ref, w_ref, h_ref, gf_ref, ys_ref, *rest):
    if final:
        yp_ref, ysamp_ref, yl_scr, o_scr, sems = rest
        dst = o_scr
    else:
        out_ref, yl_scr, sems = rest
        dst = out_ref
    b = pl.program_id(0)
    slot = b % 2

    def fetch(blk, sl):
        def per_expert(e, carry):
            _segment_copies(ys_ref, gdtab[blk, e], yl_scr.at[sl], lbtab[blk, e], ntab[blk, e], sems.at[sl], False)
            return carry

        lax.fori_loop(0, N_EXPERTS, per_expert, 0)

    @pl.when(b == 0)
    def _():
        fetch(0, 0)

    @pl.when(b + 1 < pl.num_programs(0))
    def _():
        fetch(b + 1, 1 - slot)

    pltpu.make_async_copy(ys_ref.at[_rows(0, 2 * TR)], yl_scr.at[slot], sems.at[slot]).wait()

    def merge(grp, carry):
        for lane in range(LANES):
            ya = yl_scr[slot, pl.ds(pl.multiple_of(lpos_ref[0, grp, lane], SUB), SUB), :]
            yb = yl_scr[slot, pl.ds(pl.multiple_of(lpos_ref[0, SMEM_ROWS + grp, lane], SUB), SUB), :]
            wa, wb = w_ref[0, grp, lane], w_ref[0, SMEM_ROWS + grp, lane]
            rows = _rows(grp * LANES + lane, 1)
            dst[rows, :] = h_ref[rows, :] + (wa * ya + wb * yb)
        return carry

    lax.fori_loop(0, SMEM_ROWS, merge, 0)
    if final:
        y = _rmsnorm(_load_rt(o_scr, TR), gf_ref[...])

        @pl.when(b < T_PROMPT // TR)
        def _():
            yp_ref[...] = y

        @pl.when(b >= T_PROMPT // TR)
        def _():
            ysamp_ref[...] = y


def _moe_layer(h_all, g, wr, br, layer, wg, wu, wd, gf, upper, ltri, final):
    n_r = T_ALL // TR
    h_blk = pl.BlockSpec((TR * SUB, LANES), lambda i, *_: (i, 0))
    eid, wts, cnt = pl.pallas_call(
        _router_kernel,
        grid=(n_r,),
        in_specs=[h_blk, _const((1, D_MODEL)), _const((D_MODEL, LANES)), _const((D_MODEL, LANES)),
                  _const((1, LANES))],
        out_specs=[pl.BlockSpec((2, TR), lambda i: (0, i)), pl.BlockSpec((2, TR), lambda i: (0, i)),
                   _const((N_EXPERTS, LANES))],
        out_shape=[jax.ShapeDtypeStruct((2, T_ALL), I32), jax.ShapeDtypeStruct((2, T_ALL), F32),
                   jax.ShapeDtypeStruct((N_EXPERTS, LANES), F32)],
        compiler_params=_params(),
        name="router",
    )(h_all, g, *wr, br)

    tab = jax.ShapeDtypeStruct((n_r, 1, LANES), I32)
    tab_blk = pl.BlockSpec((1, 1, LANES), lambda i: (i, 0, 0))
    lpos, ntab, lbtab, gdtab, te, nt, padstart, padlen = pl.pallas_call(
        _plan_kernel,
        grid=(n_r,),
        in_specs=[pl.BlockSpec((2, TR), lambda i: (0, i)), _const((N_EXPERTS, LANES)), _const((TR, TR)),
                  _const((N_EXPERTS, N_EXPERTS))],
        out_specs=[pl.BlockSpec((2, TR), lambda i: (0, i)), tab_blk, tab_blk, tab_blk,
                   _const((1, 256)), _const((1, LANES)), _const((1, LANES)), _const((1, LANES))],
        out_shape=[jax.ShapeDtypeStruct((2, T_ALL), I32), tab, tab, tab, jax.ShapeDtypeStruct((1, 256), I32),
                   jax.ShapeDtypeStruct((1, LANES), I32), jax.ShapeDtypeStruct((1, LANES), I32),
                   jax.ShapeDtypeStruct((1, LANES), I32)],
        scratch_shapes=[pltpu.VMEM((N_EXPERTS, LANES), F32)],
        compiler_params=_params(),
        name="plan",
    )(eid, cnt, upper, ltri)

    seg_tabs = (ntab.reshape(n_r, LANES), lbtab.reshape(n_r, LANES), gdtab.reshape(n_r, LANES))

    def per_block(a):
        return a.reshape(2, n_r, SMEM_ROWS, LANES).transpose(1, 0, 2, 3).reshape(n_r, 2 * SMEM_ROWS, LANES)

    lpos3 = per_block(lpos * SUB)
    wts3 = per_block(wts)
    smem_blk = pl.BlockSpec((1, 2 * SMEM_ROWS, LANES), lambda i, *_: (i, 0, 0), memory_space=pltpu.SMEM)

    xs = pl.pallas_call(
        _dispatch_kernel,
        grid_spec=pltpu.PrefetchScalarGridSpec(
            num_scalar_prefetch=6,
            grid=(n_r,),
            in_specs=[smem_blk, h_blk, _const((1, D_MODEL))],
            out_specs=pl.BlockSpec(memory_space=pl.ANY),
            scratch_shapes=[pltpu.VMEM((TR * SUB, LANES), F32), pltpu.VMEM((2, 2 * TR * SUB, LANES), F32),
                            pltpu.VMEM((TE * SUB, LANES), F32), pltpu.SemaphoreType.DMA((2,))]),
        out_shape=jax.ShapeDtypeStruct((P_ROWS * SUB, LANES), F32),
        compiler_params=_params(),
        name="dispatch",
    )(*seg_tabs, padstart.reshape(LANES), padlen.reshape(LANES), nt[0, 0:1], lpos3, h_all, g)

    def row_map(i, te_ref, nt_ref):
        return (jnp.minimum(i, nt_ref[0] - 1), 0)

    ys = pl.pallas_call(
        _ffn_kernel,
        grid_spec=pltpu.PrefetchScalarGridSpec(
            num_scalar_prefetch=2,
            grid=(MAX_TILES,),
            in_specs=[pl.BlockSpec((TE * SUB, LANES), row_map),
                      pl.BlockSpec((1, 1, D_MODEL, EXPERT_FF), lambda i, te_ref, nt_ref: (layer, te_ref[i], 0, 0)),
                      pl.BlockSpec((1, 1, D_MODEL, EXPERT_FF), lambda i, te_ref, nt_ref: (layer, te_ref[i], 0, 0)),
                      pl.BlockSpec((1, 1, EXPERT_FF, D_MODEL), lambda i, te_ref, nt_ref: (layer, te_ref[i], 0, 0))],
            out_specs=pl.BlockSpec((TE * SUB, LANES), lambda i, te_ref, nt_ref: (i, 0)),
            scratch_shapes=[pltpu.VMEM((D_MODEL, EXPERT_FF), BF16), pltpu.VMEM((D_MODEL, EXPERT_FF), BF16),
                            pltpu.VMEM((EXPERT_FF, D_MODEL), BF16)]),
        out_shape=jax.ShapeDtypeStruct((P_ROWS * SUB, LANES), F32),
        compiler_params=_params(),
        name="expert_ffn",
    )(te.reshape(256), nt[0, 0:1], xs, wg, wu, wd)

    n_p = T_PROMPT // TR
    if final:
        out_specs = [pl.BlockSpec((TR, D_MODEL), lambda i, *_: (jnp.minimum(i, n_p - 1), 0)),
                     pl.BlockSpec((TR, D_MODEL), lambda i, *_: (jnp.maximum(i - n_p, 0), 0))]
        out_shape = [jax.ShapeDtypeStruct((T_PROMPT, D_MODEL), F32), jax.ShapeDtypeStruct((T_SAMPLE, D_MODEL), F32)]
        scratch = [pltpu.VMEM((2, 2 * TR * SUB, LANES), F32), pltpu.VMEM((TR * SUB, LANES), F32),
                   pltpu.SemaphoreType.DMA((2,))]
    else:
        out_specs = h_blk
        out_shape = jax.ShapeDtypeStruct((T_ALL * SUB, LANES), F32)
        scratch = [pltpu.VMEM((2, 2 * TR * SUB, LANES), F32), pltpu.SemaphoreType.DMA((2,))]
    return pl.pallas_call(
        functools.partial(_combine_kernel, final),
        grid_spec=pltpu.PrefetchScalarGridSpec(
            num_scalar_prefetch=3,
            grid=(n_r,),
            in_specs=[smem_blk, smem_blk, h_blk, _const((1, D_MODEL)), pl.BlockSpec(memory_space=pl.ANY)],
            out_specs=out_specs,
            scratch_shapes=scratch),
        out_shape=out_shape,
        compiler_params=_params(),
        name="combine",
    )(*seg_tabs, lpos3, wts3, h_all, gf, ys)


def kernel(x_prompt, x_sample, state_ret, cache_conv, norm_mix_g, w_in, conv_w, conv_b, conv_ln_g, conv_ln_b,
           w_out, norm_ffn_g, router_group_w, router_group_b, router_expert_w, router_expert_b, expert_w_gate,
           expert_w_up, expert_w_down, norm_final_g):
    h = (x_prompt.reshape(T_PROMPT, D_MODEL), x_sample.reshape(T_SAMPLE, D_MODEL))

    cos_p, sin_p = _rope_tables(jnp.arange(SEQ))
    cos_s, sin_s = _rope_tables(PAST_LEN + jnp.arange(DEC_SEQ))
    reps = TM // DEC_SEQ
    tabs_p = (cos_p, sin_p) + _retention_tables(C_PROMPT, 0)
    tabs_s = (jnp.tile(cos_s, (reps, 1)), jnp.tile(sin_s, (reps, 1))) + _retention_tables(DEC_SEQ, PAST_LEN)

    upper = (jnp.arange(TR)[:, None] < jnp.arange(TR)[None, :]).astype(BF16)
    ltri = (jnp.arange(N_EXPERTS)[:, None] >= jnp.arange(N_EXPERTS)[None, :]).astype(F32)
    pad_hist = HIST - (CONV_WIDTH - 1)
    gf = norm_final_g.reshape(1, D_MODEL)

    ret_p, conv_p, ret_s, conv_s = [], [], [], []
    for l in range(DEPTH):
        cw = jnp.repeat(jnp.pad(conv_w[l], ((0, HIST - CONV_WIDTH), (0, 0))), SUB, axis=0)
        c0 = jnp.pad(cache_conv[l], ((0, 0), (pad_hist, 0), (0, 0)))
        if l == 0:
            (win_hi, win_lo), (wout_hi, wout_lo) = _split(w_in[l]), _split(w_out[l])
            lo_weights = (win_lo, wout_lo)
        else:
            win_hi, wout_hi, lo_weights = w_in[l].astype(BF16), w_out[l].astype(BF16), ()
        wr = jnp.zeros((D_MODEL, LANES), F32)
        wr = wr.at[:, 0:N_GROUPS].set(router_group_w[l])
        wr = wr.at[:, EPG:EPG + N_EXPERTS].set(router_expert_w[l].transpose(1, 0, 2).reshape(D_MODEL, N_EXPERTS))
        br = jnp.zeros((1, LANES), F32)
        br = br.at[0, 0:N_GROUPS].set(router_group_b[l])
        br = br.at[0, EPG:EPG + N_EXPERTS].set(router_expert_b[l].reshape(N_EXPERTS))
        hm, sp, cp, ss, cs = _mixer_layer(
            h, norm_mix_g[l].reshape(1, D_MODEL), win_hi, cw, conv_b[l].reshape(1, CONV_CH),
            conv_ln_g[l].reshape(1, CONV_CH), conv_ln_b[l].reshape(1, CONV_CH), wout_hi,
            state_ret[l], c0, tabs_p, tabs_s, lo_weights)
        ret_p.append(sp)
        conv_p.append(cp[:, pad_hist:])
        ret_s.append(ss)
        conv_s.append(cs[:, pad_hist:])
        h = _moe_layer(hm, norm_ffn_g[l].reshape(1, D_MODEL), _split(wr), br, l, expert_w_gate, expert_w_up,
                       expert_w_down, gf, upper, ltri, final=(l == DEPTH - 1))
        if l < DEPTH - 1:
            h = (h,)

    y_prompt = h[0].reshape(BATCH, SEQ, D_MODEL)
    y_sample = h[1].reshape(DEC_BATCH, DEC_SEQ, D_MODEL)
    return (y_prompt, y_sample, jnp.stack(ret_p), jnp.stack(conv_p), jnp.stack(ret_s), jnp.stack(conv_s))
```

```python
import functools

import jax
import jax.numpy as jnp
import numpy as np
from jax import lax
from jax.experimental import pallas as pl
from jax.experimental.pallas import tpu as pltpu

F32 = jnp.float32
BF16 = jnp.bfloat16
I32 = jnp.int32

D_MODEL = 1024
DEPTH = 2
BATCH, SEQ = 2, 8192
DEC_BATCH, DEC_SEQ = 16, 64
PAST_LEN = 1024
RET_HEADS, RET_DK, RET_DV = 4, 64, 128
QK_W = RET_HEADS * RET_DK
V_W = RET_HEADS * RET_DV
CONV_CH = 512
CONV_WIDTH = 31
HIST = 32
IN_COLS = 2 * QK_W + 2 * V_W + 2 * CONV_CH
N_GROUPS, EPG = 4, 8
N_EXPERTS = N_GROUPS * EPG
EXPERT_FF = 512
ROPE_BASE = 10000.0
EPS = 1e-6

T_PROMPT = BATCH * SEQ
T_SAMPLE = DEC_BATCH * DEC_SEQ
T_ALL = T_PROMPT + T_SAMPLE

SUB, LANES = 8, 128
TM = 512
C_PROMPT = 256
TR = 1024
TE = 512
SEG = 64
SMEM_ROWS = TR // LANES
N_SLOTS = 2 * T_ALL
MAX_TILES = N_SLOTS // TE + N_EXPERTS
P_ROWS = MAX_TILES * TE
VMEM_LIMIT = 60000 * 1024
N_PROMPT_STEPS = T_PROMPT // TM

SPLIT_FACTOR = float(2 ** 16 + 1)
_NN = (((1,), (0,)), ((), ()))
_NT = (((1,), (1,)), ((), ()))
_TN = (((0,), (0,)), ((), ()))


def _params(n_axes=1):
    return pltpu.CompilerParams(dimension_semantics=("arbitrary",) * n_axes,
                                vmem_limit_bytes=VMEM_LIMIT)


def _const(shape):
    nd = len(shape)
    return pl.BlockSpec(shape, lambda *_: (0,) * nd)


def _resident(shape):
    nd = len(shape)
    return pl.BlockSpec(shape, lambda *_: (0,) * nd, pipeline_mode=pl.Buffered(1))


def _load_rt(ref, rows):
    return jnp.concatenate([ref[pl.ds(s, rows, stride=SUB), :] for s in range(SUB)], axis=1)


def _store_rt(ref, val, rows):
    for s in range(SUB):
        ref[pl.ds(s, rows, stride=SUB), :] = val[:, s * LANES:(s + 1) * LANES]


def _split(a):
    t = a * SPLIT_FACTOR
    hi = t - (t - a)
    return hi.astype(BF16), (a - hi).astype(BF16)


def _mm(a, b, dims, precise):
    if not precise:
        return lax.dot_general(a.astype(BF16), b.astype(BF16), dims, preferred_element_type=F32)
    ah, al = _split(a)
    bh, bl = _split(b)
    return (lax.dot_general(ah, bh, dims, preferred_element_type=F32)
            + (lax.dot_general(al, bh, dims, preferred_element_type=F32)
               + lax.dot_general(ah, bl, dims, preferred_element_type=F32)))


def _mm_w(a, whi, wlo):
    if wlo is None:
        return jnp.dot(a.astype(BF16), whi, preferred_element_type=F32)
    ah, al = _split(a)
    return (jnp.dot(ah, whi, preferred_element_type=F32)
            + (jnp.dot(al, whi, preferred_element_type=F32) + jnp.dot(ah, wlo, preferred_element_type=F32)))


def _mixer_body(c, ns, carry, steps_per_seq, x_is_rt, precise,
                x_ref, g_ref, win_ref, winlo_ref, cos_ref, sin_ref, intra_ref, qdec_ref, kdec_ref, decs_ref,
                masks_ref, cw_ref, cb_ref, lng_ref, lnb_ref, wout_ref, woutlo_ref, s0_ref, c0_ref,
                out_ref, snew_ref, cnew_ref, s_scr, ubuf, p_scr, cv_scr, cat_scr):
    step = pl.program_id(0)
    x = _load_rt(x_ref, TM) if x_is_rt else x_ref[...]
    xn = x * lax.rsqrt(jnp.mean(x * x, axis=-1, keepdims=True) + EPS) * g_ref[...]
    xn_parts = _split(xn) if precise else (xn.astype(BF16), None)

    def proj(lo, hi):
        out = jnp.dot(xn_parts[0], win_ref[:, lo:hi], preferred_element_type=F32)
        if precise:
            out = out + (jnp.dot(xn_parts[1], win_ref[:, lo:hi], preferred_element_type=F32)
                         + jnp.dot(xn_parts[0], winlo_ref[:, lo:hi], preferred_element_type=F32))
        return out

    q = proj(0, QK_W)
    k = proj(QK_W, 2 * QK_W)
    v = proj(2 * QK_W, 2 * QK_W + V_W)
    gate = proj(2 * QK_W + V_W, 2 * QK_W + 2 * V_W)
    ca = proj(2 * QK_W + 2 * V_W, 2 * QK_W + 2 * V_W + CONV_CH)
    cb = proj(2 * QK_W + 2 * V_W + CONV_CH, IN_COLS)

    lane = lax.broadcasted_iota(I32, (TM, QK_W), 1)
    first_half = (lane & (RET_DK // 2)) == 0

    def rope(t):
        partner = jnp.where(first_half, pltpu.roll(t, QK_W - RET_DK // 2, 1), pltpu.roll(t, RET_DK // 2, 1))
        return t * cos_ref[...] + partner * sin_ref[...]

    q = rope(q)
    k = rope(k) * (RET_DK ** -0.5)

    if carry:
        @pl.when(step % steps_per_seq == 0)
        def _():
            s_scr[...] = jnp.zeros_like(s_scr)
            ubuf[0:HIST, :] = jnp.zeros((HIST, CONV_CH), F32)
    else:
        s_scr[...] = jnp.zeros_like(s_scr)

    lane_c = lax.broadcasted_iota(I32, (c, QK_W), 1)
    for s in range(ns):
        r0 = s * c
        qc, kc, vc = q[r0:r0 + c], k[r0:r0 + c], v[r0:r0 + c]
        if not carry:
            for h in range(RET_HEADS):
                s_scr[h * RET_DK:(h + 1) * RET_DK, h * RET_DV:(h + 1) * RET_DV] = s0_ref[s, h]
        state = s_scr[...]
        o_cross = _mm(qc, state, _NN, precise) * qdec_ref[...]
        heads = []
        for h in range(RET_HEADS):
            qh = jnp.where((lane_c >> 6) == h, qc, 0.0)
            scores = _mm(qh, kc, _NT, precise) * intra_ref[h]
            heads.append(_mm(scores, vc[:, h * RET_DV:(h + 1) * RET_DV], _NN, precise))
        o = jnp.concatenate(heads, axis=1) + o_cross
        kv = _mm(kc * kdec_ref[...], vc, _TN, precise)
        s_new = decs_ref[...] * state + masks_ref[...] * kv
        if carry:
            s_scr[...] = s_new
            if s == ns - 1:
                @pl.when(step % steps_per_seq == steps_per_seq - 1)
                def _():
                    for h in range(RET_HEADS):
                        snew_ref[0, h] = s_new[h * RET_DK:(h + 1) * RET_DK, h * RET_DV:(h + 1) * RET_DV]
        else:
            for h in range(RET_HEADS):
                snew_ref[s, h] = s_new[h * RET_DK:(h + 1) * RET_DK, h * RET_DV:(h + 1) * RET_DV]

        normed = []
        for h in range(RET_HEADS):
            oh = o[:, h * RET_DV:(h + 1) * RET_DV]
            normed.append(oh * lax.rsqrt(jnp.mean(oh * oh, axis=-1, keepdims=True) + EPS))
        gc = gate[r0:r0 + c]
        og = jnp.concatenate(normed, axis=1) * (gc * jax.nn.sigmoid(gc))

        u = ca[r0:r0 + c] * jax.nn.sigmoid(cb[r0:r0 + c])
        if not carry:
            ubuf[0:HIST, :] = c0_ref[s]
        ubuf[HIST:HIST + c, :] = u
        off = HIST - (CONV_WIDTH - 1)
        for b in range(SUB):
            nrow = c + (2 * SUB if off + b + c - 1 >= c + SUB else SUB)
            for rb in range(0, nrow, 32):
                nr = min(32, nrow - rb)
                acc = None
                for j in range(b, CONV_WIDTH, SUB):
                    wj = jnp.concatenate([cw_ref[j * SUB:(j + 1) * SUB, :]] * (nr // SUB), axis=0)
                    term = ubuf[rb + j - b:rb + j - b + nr, :] * wj
                    acc = term if acc is None else acc + term
                p_scr[b, rb:rb + nr, :] = acc
        for rb in range(0, c, 32):
            acc = jnp.broadcast_to(cb_ref[...], (32, CONV_CH))
            for b in range(SUB):
                acc = acc + p_scr[b, rb + off + b:rb + off + b + 32, :]
            cv_scr[rb:rb + 32, :] = acc
        tail = ubuf[c:c + HIST, :]
        if carry:
            ubuf[0:HIST, :] = tail
            if s == ns - 1:
                @pl.when(step % steps_per_seq == steps_per_seq - 1)
                def _():
                    cnew_ref[0] = tail
        else:
            cnew_ref[s] = tail
        cv = cv_scr[0:c, :]
        mu = jnp.mean(cv, axis=-1, keepdims=True)
        xc = cv - mu
        var = jnp.mean(xc * xc, axis=-1, keepdims=True)
        cn = xc * lax.rsqrt(var + EPS) * lng_ref[...] + lnb_ref[...]
        cact = cn * jax.nn.sigmoid(cn)
        cat_scr[r0:r0 + c, 0:V_W] = og.astype(cat_scr.dtype)
        cat_scr[r0:r0 + c, V_W:D_MODEL] = cact.astype(cat_scr.dtype)

    y = _mm_w(cat_scr[...], wout_ref[...], woutlo_ref[...] if precise else None)
    _store_rt(out_ref, x + y, TM)


def _mixer_kernel(x_is_rt, precise, *refs):
    if x_is_rt:
        xp_ref = xs_ref = refs[0]
        refs = refs[1:]
    else:
        xp_ref, xs_ref = refs[0], refs[1]
        refs = refs[2:]
    if precise:
        winlo_ref, woutlo_ref = refs[0], refs[1]
        refs = refs[2:]
    else:
        winlo_ref = woutlo_ref = None
    (g_ref, win_ref, cw_ref, cb_ref, lng_ref, lnb_ref, wout_ref, masks_ref,
     cos_p, sin_p, intra_p, qdec_p, kdec_p, decs_p,
     cos_s, sin_s, intra_s, qdec_s, kdec_s, decs_s, s0_ref, c0_ref,
     out_ref, snew_p, cnew_p, snew_s, cnew_s, s_scr, ubuf, p_scr, cv_scr, cat_scr) = refs
    step = pl.program_id(0)

    @pl.when(step < N_PROMPT_STEPS)
    def _():
        _mixer_body(C_PROMPT, TM // C_PROMPT, True, SEQ // TM, x_is_rt, precise,
                    xp_ref, g_ref, win_ref, winlo_ref, cos_p, sin_p, intra_p, qdec_p, kdec_p, decs_p,
                    masks_ref, cw_ref, cb_ref, lng_ref, lnb_ref, wout_ref, woutlo_ref, None, None,
                    out_ref, snew_p, cnew_p, s_scr, ubuf, p_scr, cv_scr, cat_scr)

    @pl.when(step >= N_PROMPT_STEPS)
    def _():
        _mixer_body(DEC_SEQ, TM // DEC_SEQ, False, 1, x_is_rt, precise,
                    xs_ref, g_ref, win_ref, winlo_ref, cos_s, sin_s, intra_s, qdec_s, kdec_s, decs_s,
                    masks_ref, cw_ref, cb_ref, lng_ref, lnb_ref, wout_ref, woutlo_ref, s0_ref, c0_ref,
                    out_ref, snew_s, cnew_s, s_scr, ubuf, p_scr, cv_scr, cat_scr)


def _retention_tables(c):
    log_g = np.log1p(-np.exp2(-5.0 - np.arange(RET_HEADS, dtype=np.float64)))
    idx = np.arange(c, dtype=np.float64)
    rel = idx[:, None] - idx[None, :]
    intra = np.where(rel >= 0, np.exp(log_g[:, None, None] * np.maximum(rel, 0.0)), 0.0)
    q_dec = np.exp(log_g[None, :] * (idx + 1.0)[:, None])
    k_dec = np.exp(log_g[None, :] * (c - 1.0 - idx)[:, None])
    chunk_dec = np.exp(log_g * c)
    qdec = np.repeat(q_dec, RET_DV, axis=1)
    kdec = np.repeat(k_dec, RET_DK, axis=1)
    row_h = np.arange(QK_W) // RET_DK
    col_h = np.arange(V_W) // RET_DV
    masks = (row_h[:, None] == col_h[None, :]).astype(np.float64)
    decs = masks * chunk_dec[row_h][:, None]
    return tuple(t.astype(np.float32) for t in (intra, qdec, kdec, decs, masks))


def _rope_tables(pos):
    half = RET_DK // 2
    inv = ROPE_BASE ** (-np.arange(half, dtype=np.float64) / half)
    ang = pos.astype(np.float64)[:, None] * inv[None, :]
    cos, sin = np.cos(ang), np.sin(ang)
    cos_t = np.tile(np.concatenate([cos, cos], axis=1), (1, RET_HEADS))
    sin_t = np.tile(np.concatenate([-sin, sin], axis=1), (1, RET_HEADS))
    return cos_t.astype(np.float32), sin_t.astype(np.float32)


def _mixer_layer(xs, g, win, cw, cb, lng, lnb, wout, s0, c0, tabs_p, tabs_s, lo_weights=()):
    precise = len(lo_weights) == 2
    lo_specs = [_resident((D_MODEL, IN_COLS)), _resident((D_MODEL, D_MODEL))] if precise else []
    x_is_rt = len(xs) == 1
    if x_is_rt:
        x_specs = [pl.BlockSpec((TM * SUB, LANES), lambda i: (i, 0))]
    else:
        x_specs = [pl.BlockSpec((TM, D_MODEL), lambda i: (jnp.minimum(i, N_PROMPT_STEPS - 1), 0)),
                   pl.BlockSpec((TM, D_MODEL), lambda i: (jnp.maximum(i - N_PROMPT_STEPS, 0), 0))]
    cos_p, sin_p, intra_p, qdec_p, kdec_p, decs_p, masks = tabs_p
    cos_s, sin_s, intra_s, qdec_s, kdec_s, decs_s, _ = tabs_s
    steps = SEQ // TM
    ns = TM // DEC_SEQ

    def table_specs(c):
        return [_const((RET_HEADS, c, c)), _const((c, V_W)), _const((c, QK_W)), _const((QK_W, V_W))]

    def sample_blk(i):
        return jnp.maximum(i - N_PROMPT_STEPS, 0)

    def prompt_blk(i):
        return jnp.minimum(i // steps, BATCH - 1)

    def prompt_pos(i):
        return (jnp.minimum(i, N_PROMPT_STEPS - 1) % steps, 0)

    return pl.pallas_call(
        functools.partial(_mixer_kernel, x_is_rt, precise),
        grid=(T_ALL // TM,),
        in_specs=x_specs + lo_specs + [_const((1, D_MODEL)), _resident((D_MODEL, IN_COLS)),
                  _const((HIST * SUB, CONV_CH)), _const((1, CONV_CH)), _const((1, CONV_CH)), _const((1, CONV_CH)),
                  _resident((D_MODEL, D_MODEL)), _const((QK_W, V_W)),
                  pl.BlockSpec((TM, QK_W), prompt_pos), pl.BlockSpec((TM, QK_W), prompt_pos)]
        + table_specs(C_PROMPT) + [_const((TM, QK_W)), _const((TM, QK_W))] + table_specs(DEC_SEQ)
        + [pl.BlockSpec((ns, RET_HEADS, RET_DK, RET_DV), lambda i: (sample_blk(i), 0, 0, 0)),
           pl.BlockSpec((ns, HIST, CONV_CH), lambda i: (sample_blk(i), 0, 0))],
        out_specs=[pl.BlockSpec((TM * SUB, LANES), lambda i: (i, 0)),
                   pl.BlockSpec((1, RET_HEADS, RET_DK, RET_DV), lambda i: (prompt_blk(i), 0, 0, 0)),
                   pl.BlockSpec((1, HIST, CONV_CH), lambda i: (prompt_blk(i), 0, 0)),
                   pl.BlockSpec((ns, RET_HEADS, RET_DK, RET_DV), lambda i: (sample_blk(i), 0, 0, 0)),
                   pl.BlockSpec((ns, HIST, CONV_CH), lambda i: (sample_blk(i), 0, 0))],
        out_shape=[jax.ShapeDtypeStruct((T_ALL * SUB, LANES), F32),
                   jax.ShapeDtypeStruct((BATCH, RET_HEADS, RET_DK, RET_DV), F32),
                   jax.ShapeDtypeStruct((BATCH, HIST, CONV_CH), F32),
                   jax.ShapeDtypeStruct((DEC_BATCH, RET_HEADS, RET_DK, RET_DV), F32),
                   jax.ShapeDtypeStruct((DEC_BATCH, HIST, CONV_CH), F32)],
        scratch_shapes=[pltpu.VMEM((QK_W, V_W), F32), pltpu.VMEM((HIST + C_PROMPT, CONV_CH), F32),
                        pltpu.VMEM((SUB, C_PROMPT + 2 * SUB, CONV_CH), F32),
                        pltpu.VMEM((C_PROMPT, CONV_CH), F32),
                        pltpu.VMEM((TM, D_MODEL), F32 if precise else BF16)],
        compiler_params=_params(),
        name="mixer",
    )(*xs, *lo_weights, g, win, cw, cb, lng, lnb, wout, masks, cos_p, sin_p, intra_p, qdec_p, kdec_p, decs_p,
      cos_s, sin_s, intra_s, qdec_s, kdec_s, decs_s, s0, c0)


def _rmsnorm(x, g):
    return x * lax.rsqrt(jnp.mean(x * x, axis=-1, keepdims=True) + EPS) * g


def _store_per_token(ref, slot_a, slot_b):
    for k, val in enumerate((slot_a, slot_b)):
        for grp in range(SMEM_ROWS):
            ref[0, k * SMEM_ROWS + grp:k * SMEM_ROWS + grp + 1, :] = val[:, grp * LANES:(grp + 1) * LANES]


def _router_kernel(h_ref, g_ref, wrhi_ref, wrlo_ref, br_ref, eid_ref, wts_ref, cnt_ref):
    n = TR
    xn = _rmsnorm(_load_rt(h_ref, TR), g_ref[...])
    logits = (_mm_w(xn, wrhi_ref[...], wrlo_ref[...]) + br_ref[...]).T
    best = logits[0:1, :]
    gidx = jnp.zeros((1, n), I32)
    for r in range(1, N_GROUPS):
        row = logits[r:r + 1, :]
        upd = row > best
        gidx = jnp.where(upd, r, gidx)
        best = jnp.where(upd, row, best)
    denom = jnp.zeros((1, n), F32)
    for r in range(N_GROUPS):
        denom = denom + jnp.exp(logits[r:r + 1, :] - best)
    pgate = 1.0 / denom
    le = logits[EPG:2 * EPG, :]
    for gi in range(1, N_GROUPS):
        le = jnp.where(gidx == gi, logits[EPG * (gi + 1):EPG * (gi + 2), :], le)
    sub = lax.broadcasted_iota(I32, (EPG, n), 0)
    v1 = jnp.max(le, axis=0, keepdims=True)
    i1 = jnp.min(jnp.where(le == v1, sub, EPG), axis=0, keepdims=True)
    le2 = jnp.where(sub == i1, -jnp.inf, le)
    v2 = jnp.max(le2, axis=0, keepdims=True)
    i2 = jnp.min(jnp.where(le2 == v2, sub, EPG), axis=0, keepdims=True)
    e2 = jnp.exp(v2 - v1)
    den2 = 1.0 + e2
    wa = (1.0 / den2) * pgate
    wb = (e2 / den2) * pgate
    e1 = gidx * EPG + i1
    e2 = gidx * EPG + i2
    eid_ref[0:1, :] = e1
    eid_ref[1:2, :] = e2
    _store_per_token(wts_ref, wa, wb)
    experts = lax.broadcasted_iota(I32, (N_EXPERTS, n), 0)
    hits = (experts == e1).astype(F32) + (experts == e2).astype(F32)

    @pl.when(pl.program_id(0) == 0)
    def _():
        cnt_ref[...] = jnp.zeros_like(cnt_ref)

    cnt_ref[...] = cnt_ref[...] + jnp.sum(hits, axis=1, keepdims=True)


def _plan_kernel(eid_ref, cnt_ref, upper_ref, ltri_ref, lpos_ref, ntab_ref, lbtab_ref, gdtab_ref, te_ref, nt_ref,
                 padstart_ref, padlen_ref, run_scr):
    i = pl.program_id(0)
    rows = lax.broadcasted_iota(I32, (N_EXPERTS, TR), 0)
    oh0 = (rows == eid_ref[0:1, :]).astype(F32)
    oh1 = (rows == eid_ref[1:2, :]).astype(F32)
    tot0 = jnp.sum(oh0, axis=1, keepdims=True)
    tot1 = jnp.sum(oh1, axis=1, keepdims=True)
    eye = (lax.broadcasted_iota(I32, (N_EXPERTS, LANES), 0)
           == lax.broadcasted_iota(I32, (N_EXPERTS, LANES), 1)).astype(F32)

    def to_row(col):
        return jnp.sum(col * eye, axis=0, keepdims=True).astype(I32)

    @pl.when(i == 0)
    def _():
        cnt = cnt_ref[...].astype(I32)
        ntile = ((cnt + (TE - 1)) >> (TE.bit_length() - 1)).astype(F32)
        incl = jnp.dot(ltri_ref[...], ntile, preferred_element_type=F32,
                       precision=lax.Precision.HIGHEST)
        run_scr[...] = (incl - ntile) * float(TE)
        ntot = incl[N_EXPERTS - 1:N_EXPERTS, 0:1]
        j = lax.broadcasted_iota(I32, (N_EXPERTS, 256), 1).astype(F32)
        j = jnp.minimum(j, ntot - 1.0)
        te = jnp.sum((incl[:, 0:1] <= j).astype(F32), axis=0, keepdims=True)
        te_ref[...] = te.astype(I32)
        nt_ref[...] = incl[N_EXPERTS - 1:N_EXPERTS, :].astype(I32)
        cntf = cnt_ref[:, 0:1]
        padstart_ref[...] = to_row((incl[:, 0:1] - ntile[:, 0:1]) * float(TE) + cntf)
        padlen_ref[...] = to_row(ntile[:, 0:1] * float(TE) - cntf)

    run = run_scr[:, 0:1]
    tot = tot0 + tot1
    lbase = jnp.dot(ltri_ref[...], jnp.broadcast_to(tot, (N_EXPERTS, LANES)), preferred_element_type=F32,
                    precision=lax.Precision.HIGHEST)[:, 0:1] - tot
    ex0 = jnp.dot(oh0.astype(BF16), upper_ref[...], preferred_element_type=F32)
    ex1 = jnp.dot(oh1.astype(BF16), upper_ref[...], preferred_element_type=F32)
    p0 = jnp.sum(oh0 * (lbase + ex0), axis=0, keepdims=True)
    p1 = jnp.sum(oh1 * (lbase + tot0 + ex1), axis=0, keepdims=True)
    _store_per_token(lpos_ref, p0.astype(I32) * SUB, p1.astype(I32) * SUB)
    ntab_ref[0] = to_row(tot)
    lbtab_ref[0] = to_row(lbase)
    gdtab_ref[0] = to_row(run)
    run_scr[...] = run_scr[...] + tot


def _rows(start, n):
    return pl.ds(pl.multiple_of(start * SUB, SUB), n * SUB)


def _segment_copies(src_ref, src0, dst_ref, dst0, n, sem, wait):
    def piece(off, size):
        cp = pltpu.make_async_copy(src_ref.at[_rows(src0 + off, size)], dst_ref.at[_rows(dst0 + off, size)], sem)
        if wait:
            cp.wait()
        else:
            cp.start()

    nfull = n >> (SEG.bit_length() - 1)

    def full(j, carry):
        piece(j * SEG, SEG)
        return carry

    lax.fori_loop(0, nfull, full, 0)
    rem = n - nfull * SEG
    bit = SEG // 2
    while bit >= 1:
        @pl.when((rem & bit) != 0)
        def _(bit=bit):
            piece(nfull * SEG + (rem & ~(2 * bit - 1)), bit)
        bit //= 2


def _dispatch_kernel(ntab, lbtab, gdtab, padstart, padlen, nt, lpos_ref, h_ref, g_ref, xs_ref,
                     xn_scr, ls_scr, zero_scr, sems):
    b = pl.program_id(0)
    last = pl.num_programs(0) - 1
    slot = b % 2
    _store_rt(xn_scr, _rmsnorm(_load_rt(h_ref, TR), g_ref[...]), TR)

    def place(grp, carry):
        for lane in range(LANES):
            row = xn_scr[_rows(grp * LANES + lane, 1), :]
            for k in range(2):
                dst = pl.multiple_of(lpos_ref[0, k * SMEM_ROWS + grp, lane], SUB)
                ls_scr[slot, pl.ds(dst, SUB), :] = row
        return carry

    lax.fori_loop(0, SMEM_ROWS, place, 0)

    def per_expert(e, carry):
        _segment_copies(ls_scr.at[slot], lbtab[b, e], xs_ref, gdtab[b, e], ntab[b, e], sems.at[slot], False)
        return carry

    lax.fori_loop(0, N_EXPERTS, per_expert, 0)

    def wait_block(sl):
        pltpu.make_async_copy(ls_scr.at[sl], xs_ref.at[_rows(0, 2 * TR)], sems.at[sl]).wait()

    @pl.when(b > 0)
    def _():
        wait_block(1 - slot)

    @pl.when(b == last)
    def _():
        wait_block(slot)
        zero_scr[...] = jnp.zeros_like(zero_scr)
        for wait in (False, True):
            def per_expert(e, carry, wait=wait):
                _segment_copies(zero_scr, 0, xs_ref, padstart[e], padlen[e], sems.at[0], wait)
                return carry

            lax.fori_loop(0, N_EXPERTS, per_expert, 0)

            def per_tile(j, carry, wait=wait):
                cp = pltpu.make_async_copy(zero_scr, xs_ref.at[_rows(j * TE, TE)], sems.at[0])
                if wait:
                    cp.wait()
                else:
                    cp.start()
                return carry

            lax.fori_loop(nt[0], MAX_TILES, per_tile, 0)


def _ffn_kernel(te_ref, nt_ref, xs_ref, wg_ref, wu_ref, wd_ref, ys_ref, wg_scr, wu_scr, wd_scr):
    i = pl.program_id(0)
    first = (i == 0) | (te_ref[i] != te_ref[jnp.maximum(i - 1, 0)])

    @pl.when(first)
    def _():
        wg_scr[...] = wg_ref[0, 0].astype(BF16)
        wu_scr[...] = wu_ref[0, 0].astype(BF16)
        wd_scr[...] = wd_ref[0, 0].astype(BF16)

    @pl.when(i < nt_ref[0])
    def _():
        x = _load_rt(xs_ref, TE).astype(BF16)
        a = jnp.dot(x, wg_scr[...], preferred_element_type=F32)
        b = jnp.dot(x, wu_scr[...], preferred_element_type=F32)
        hdn = (a * jax.nn.sigmoid(a)) * b
        _store_rt(ys_ref, jnp.dot(hdn.astype(BF16), wd_scr[...], preferred_element_type=F32), TE)

    @pl.when(i >= nt_ref[0])
    def _():
        ys_ref[...] = jnp.zeros_like(ys_ref)


def _combine_kernel(final, ntab, lbtab, gdtab, lpos_ref, w_ref, h_ref, gf_ref, ys_ref, *rest):
    if final:
        yp_ref, ysamp_ref, yl_scr, o_scr, sems = rest
        dst = o_scr
    else:
        out_ref, yl_scr, sems = rest
        dst = out_ref
    b = pl.program_id(0)
    slot = b % 2

    def fetch(blk, sl):
        def per_expert(e, carry):
            _segment_copies(ys_ref, gdtab[blk, e], yl_scr.at[sl], lbtab[blk, e], ntab[blk, e], sems.at[sl], False)
            return carry

        lax.fori_loop(0, N_EXPERTS, per_expert, 0)

    @pl.when(b == 0)
    def _():
        fetch(0, 0)

    @pl.when(b + 1 < pl.num_programs(0))
    def _():
        fetch(b + 1, 1 - slot)

    pltpu.make_async_copy(ys_ref.at[_rows(0, 2 * TR)], yl_scr.at[slot], sems.at[slot]).wait()

    def merge(grp, carry):
        for lane in range(LANES):
            ya = yl_scr[slot, pl.ds(pl.multiple_of(lpos_ref[0, grp, lane], SUB), SUB), :]
            yb = yl_scr[slot, pl.ds(pl.multiple_of(lpos_ref[0, SMEM_ROWS + grp, lane], SUB), SUB), :]
            wa, wb = w_ref[0, grp, lane], w_ref[0, SMEM_ROWS + grp, lane]
            rows = _rows(grp * LANES + lane, 1)
            dst[rows, :] = h_ref[rows, :] + (wa * ya + wb * yb)
        return carry

    lax.fori_loop(0, SMEM_ROWS, merge, 0)
    if final:
        y = _rmsnorm(_load_rt(o_scr, TR), gf_ref[...])

        @pl.when(b < T_PROMPT // TR)
        def _():
            yp_ref[...] = y

        @pl.when(b >= T_PROMPT // TR)
        def _():
            ysamp_ref[...] = y


def _moe_layer(h_all, g, wr, br, layer, wg, wu, wd, gf, upper, ltri, final):
    n_r = T_ALL // TR
    h_blk = pl.BlockSpec((TR * SUB, LANES), lambda i, *_: (i, 0))
    per_token_blk = pl.BlockSpec((1, 2 * SMEM_ROWS, LANES), lambda i: (i, 0, 0))
    eid, wts3, cnt = pl.pallas_call(
        _router_kernel,
        grid=(n_r,),
        in_specs=[h_blk, _const((1, D_MODEL)), _const((D_MODEL, LANES)), _const((D_MODEL, LANES)),
                  _const((1, LANES))],
        out_specs=[pl.BlockSpec((2, TR), lambda i: (0, i)), per_token_blk, _const((N_EXPERTS, LANES))],
        out_shape=[jax.ShapeDtypeStruct((2, T_ALL), I32), jax.ShapeDtypeStruct((n_r, 2 * SMEM_ROWS, LANES), F32),
                   jax.ShapeDtypeStruct((N_EXPERTS, LANES), F32)],
        compiler_params=_params(),
        name="router",
    )(h_all, g, *wr, br)

    tab = jax.ShapeDtypeStruct((n_r, 1, LANES), I32)
    tab_blk = pl.BlockSpec((1, 1, LANES), lambda i: (i, 0, 0))
    lpos3, ntab, lbtab, gdtab, te, nt, padstart, padlen = pl.pallas_call(
        _plan_kernel,
        grid=(n_r,),
        in_specs=[pl.BlockSpec((2, TR), lambda i: (0, i)), _const((N_EXPERTS, LANES)), _const((TR, TR)),
                  _const((N_EXPERTS, N_EXPERTS))],
        out_specs=[per_token_blk, tab_blk, tab_blk, tab_blk,
                   _const((1, 256)), _const((1, LANES)), _const((1, LANES)), _const((1, LANES))],
        out_shape=[jax.ShapeDtypeStruct((n_r, 2 * SMEM_ROWS, LANES), I32), tab, tab, tab,
                   jax.ShapeDtypeStruct((1, 256), I32),
                   jax.ShapeDtypeStruct((1, LANES), I32), jax.ShapeDtypeStruct((1, LANES), I32),
                   jax.ShapeDtypeStruct((1, LANES), I32)],
        scratch_shapes=[pltpu.VMEM((N_EXPERTS, LANES), F32)],
        compiler_params=_params(),
        name="plan",
    )(eid, cnt, upper, ltri)

    seg_tabs = (ntab.reshape(n_r, LANES), lbtab.reshape(n_r, LANES), gdtab.reshape(n_r, LANES))
    smem_blk = pl.BlockSpec((1, 2 * SMEM_ROWS, LANES), lambda i, *_: (i, 0, 0), memory_space=pltpu.SMEM)

    xs = pl.pallas_call(
        _dispatch_kernel,
        grid_spec=pltpu.PrefetchScalarGridSpec(
            num_scalar_prefetch=6,
            grid=(n_r,),
            in_specs=[smem_blk, h_blk, _const((1, D_MODEL))],
            out_specs=pl.BlockSpec(memory_space=pl.ANY),
            scratch_shapes=[pltpu.VMEM((TR * SUB, LANES), F32), pltpu.VMEM((2, 2 * TR * SUB, LANES), F32),
                            pltpu.VMEM((TE * SUB, LANES), F32), pltpu.SemaphoreType.DMA((2,))]),
        out_shape=jax.ShapeDtypeStruct((P_ROWS * SUB, LANES), F32),
        compiler_params=_params(),
        name="dispatch",
    )(*seg_tabs, padstart.reshape(LANES), padlen.reshape(LANES), nt[0, 0:1], lpos3, h_all, g)

    def row_map(i, te_ref, nt_ref):
        return (jnp.minimum(i, nt_ref[0] - 1), 0)

    ys = pl.pallas_call(
        _ffn_kernel,
        grid_spec=pltpu.PrefetchScalarGridSpec(
            num_scalar_prefetch=2,
            grid=(MAX_TILES,),
            in_specs=[pl.BlockSpec((TE * SUB, LANES), row_map),
                      pl.BlockSpec((1, 1, D_MODEL, EXPERT_FF), lambda i, te_ref, nt_ref: (layer, te_ref[i], 0, 0)),
                      pl.BlockSpec((1, 1, D_MODEL, EXPERT_FF), lambda i, te_ref, nt_ref: (layer, te_ref[i], 0, 0)),
                      pl.BlockSpec((1, 1, EXPERT_FF, D_MODEL), lambda i, te_ref, nt_ref: (layer, te_ref[i], 0, 0))],
            out_specs=pl.BlockSpec((TE * SUB, LANES), lambda i, te_ref, nt_ref: (i, 0)),
            scratch_shapes=[pltpu.VMEM((D_MODEL, EXPERT_FF), BF16), pltpu.VMEM((D_MODEL, EXPERT_FF), BF16),
                            pltpu.VMEM((EXPERT_FF, D_MODEL), BF16)]),
        out_shape=jax.ShapeDtypeStruct((P_ROWS * SUB, LANES), F32),
        compiler_params=_params(),
        name="expert_ffn",
    )(te.reshape(256), nt[0, 0:1], xs, wg, wu, wd)

    n_p = T_PROMPT // TR
    if final:
        out_specs = [pl.BlockSpec((TR, D_MODEL), lambda i, *_: (jnp.minimum(i, n_p - 1), 0)),
                     pl.BlockSpec((TR, D_MODEL), lambda i, *_: (jnp.maximum(i - n_p, 0), 0))]
        out_shape = [jax.ShapeDtypeStruct((T_PROMPT, D_MODEL), F32), jax.ShapeDtypeStruct((T_SAMPLE, D_MODEL), F32)]
        scratch = [pltpu.VMEM((2, 2 * TR * SUB, LANES), F32), pltpu.VMEM((TR * SUB, LANES), F32),
                   pltpu.SemaphoreType.DMA((2,))]
    else:
        out_specs = h_blk
        out_shape = jax.ShapeDtypeStruct((T_ALL * SUB, LANES), F32)
        scratch = [pltpu.VMEM((2, 2 * TR * SUB, LANES), F32), pltpu.SemaphoreType.DMA((2,))]
    return pl.pallas_call(
        functools.partial(_combine_kernel, final),
        grid_spec=pltpu.PrefetchScalarGridSpec(
            num_scalar_prefetch=3,
            grid=(n_r,),
            in_specs=[smem_blk, smem_blk, h_blk, _const((1, D_MODEL)), pl.BlockSpec(memory_space=pl.ANY)],
            out_specs=out_specs,
            scratch_shapes=scratch),
        out_shape=out_shape,
        compiler_params=_params(),
        name="combine",
    )(*seg_tabs, lpos3, wts3, h_all, gf, ys)


def kernel(x_prompt, x_sample, state_ret, cache_conv, norm_mix_g, w_in, conv_w, conv_b, conv_ln_g, conv_ln_b,
           w_out, norm_ffn_g, router_group_w, router_group_b, router_expert_w, router_expert_b, expert_w_gate,
           expert_w_up, expert_w_down, norm_final_g):
    h = (x_prompt.reshape(T_PROMPT, D_MODEL), x_sample.reshape(T_SAMPLE, D_MODEL))

    cos_p, sin_p = _rope_tables(np.arange(SEQ))
    cos_s, sin_s = _rope_tables(PAST_LEN + np.arange(DEC_SEQ))
    reps = TM // DEC_SEQ
    tabs_p = (cos_p, sin_p) + _retention_tables(C_PROMPT)
    tabs_s = (np.tile(cos_s, (reps, 1)), np.tile(sin_s, (reps, 1))) + _retention_tables(DEC_SEQ)

    upper = jnp.asarray(np.arange(TR)[:, None] < np.arange(TR)[None, :], BF16)
    ltri = (np.arange(N_EXPERTS)[:, None] >= np.arange(N_EXPERTS)[None, :]).astype(np.float32)
    pad_hist = HIST - (CONV_WIDTH - 1)
    gf = norm_final_g.reshape(1, D_MODEL)

    ret_p, conv_p, ret_s, conv_s = [], [], [], []
    for l in range(DEPTH):
        cw = jnp.repeat(jnp.pad(conv_w[l], ((0, HIST - CONV_WIDTH), (0, 0))), SUB, axis=0)
        c0 = jnp.pad(cache_conv[l], ((0, 0), (pad_hist, 0), (0, 0)))
        if l == 0:
            (win_hi, win_lo), (wout_hi, wout_lo) = _split(w_in[l]), _split(w_out[l])
            lo_weights = (win_lo, wout_lo)
        else:
            win_hi, wout_hi, lo_weights = w_in[l].astype(BF16), w_out[l].astype(BF16), ()
        wr = jnp.zeros((D_MODEL, LANES), F32)
        wr = wr.at[:, 0:N_GROUPS].set(router_group_w[l])
        wr = wr.at[:, EPG:EPG + N_EXPERTS].set(router_expert_w[l].transpose(1, 0, 2).reshape(D_MODEL, N_EXPERTS))
        br = jnp.zeros((1, LANES), F32)
        br = br.at[0, 0:N_GROUPS].set(router_group_b[l])
        br = br.at[0, EPG:EPG + N_EXPERTS].set(router_expert_b[l].reshape(N_EXPERTS))
        hm, sp, cp, ss, cs = _mixer_layer(
            h, norm_mix_g[l].reshape(1, D_MODEL), win_hi, cw, conv_b[l].reshape(1, CONV_CH),
            conv_ln_g[l].reshape(1, CONV_CH), conv_ln_b[l].reshape(1, CONV_CH), wout_hi,
            state_ret[l], c0, tabs_p, tabs_s, lo_weights)
        ret_p.append(sp)
        conv_p.append(cp[:, pad_hist:])
        ret_s.append(ss)
        conv_s.append(cs[:, pad_hist:])
        h = _moe_layer(hm, norm_ffn_g[l].reshape(1, D_MODEL), _split(wr), br, l, expert_w_gate, expert_w_up,
                       expert_w_down, gf, upper, ltri, final=(l == DEPTH - 1))
        if l < DEPTH - 1:
            h = (h,)

    y_prompt = h[0].reshape(BATCH, SEQ, D_MODEL)
    y_sample = h[1].reshape(DEC_BATCH, DEC_SEQ, D_MODEL)
    return (y_prompt, y_sample, jnp.stack(ret_p), jnp.stack(conv_p), jnp.stack(ret_s), jnp.stack(conv_s))
```
